```python
import math
import jax, jax.numpy as jnp
from jax import lax
import numpy as np

D_MODEL = 1024
BATCH = 2
SEQ = 16384
DEPTH = 2

CHUNK = 64
N_MIXERS = 2
N_A_LAYERS = (DEPTH + 1) // 2
N_B_LAYERS = DEPTH // 2
RMS_EPS = 1e-6
LN_EPS = 1e-5

A_HEADS = 8
A_HEAD_DIM = 128
A_WIDTH = A_HEADS * A_HEAD_DIM
IDX_HEADS = 8
IDX_DIM = 64
TOPK_MAX = 256
Q_BLOCK = 128
A_IN_COLS = 4 * A_WIDTH + IDX_HEADS * IDX_DIM + IDX_DIM + IDX_HEADS

REL_BUCKETS = 32
REL_MAX_DIST = 128

SGU_CHUNK = 128
B_WIDTH = 2 * D_MODEL
B_GROUPS = 8
B_GROUP_DIM = B_WIDTH // B_GROUPS
B_IN_COLS = 3 * B_WIDTH

kernel_name = "hybrid_dsa_sgu_streaming_encoder"


def rmsnorm(x, g):
    x32 = x.astype(jnp.float32)
    y = x32 * lax.rsqrt(jnp.mean(x32 * x32, axis=-1, keepdims=True) + RMS_EPS)
    return y.astype(x.dtype) * g


def layernorm(x, g, b):
    x32 = x.astype(jnp.float32)
    mu = jnp.mean(x32, axis=-1, keepdims=True)
    var = jnp.mean(jnp.square(x32 - mu), axis=-1, keepdims=True)
    return ((x32 - mu) * lax.rsqrt(var + LN_EPS)).astype(x.dtype) * g + b


def t5_bucket(rel):
    half = REL_BUCKETS // 2
    max_exact = half // 2
    ret = jnp.where(rel < 0, half, 0)
    n = jnp.abs(rel)
    nf = jnp.maximum(n, 1).astype(jnp.float32)
    large = max_exact + (jnp.log(nf / max_exact) / math.log(REL_MAX_DIST / max_exact)
                         * (half - max_exact)).astype(jnp.int32)
    large = jnp.minimum(large, half - 1)
    return ret + jnp.where(n < max_exact, n, large)


def dsa_mixer(h, w_in, w_out, rel_bias):
    bsz, seq, _ = h.shape
    offs = np.cumsum([A_WIDTH, A_WIDTH, A_WIDTH, A_WIDTH,
                      IDX_HEADS * IDX_DIM, IDX_DIM]).tolist()
    q, k, v, z, qi, ki, wi = jnp.split(h @ w_in, offs, axis=-1)
    q = q.reshape(bsz, seq, A_HEADS, A_HEAD_DIM)
    k = k.reshape(bsz, seq, A_HEADS, A_HEAD_DIM)
    v = v.reshape(bsz, seq, A_HEADS, A_HEAD_DIM)
    qi = qi.reshape(bsz, seq, IDX_HEADS, IDX_DIM).astype(jnp.float32)
    ki = ki.astype(jnp.float32)
    wi = wi.astype(jnp.float32) * (IDX_HEADS ** -0.5)

    topk = min(TOPK_MAX, seq // 4)
    n_blk = seq // Q_BLOCK
    key_chunk = jnp.arange(seq, dtype=jnp.int32) // CHUNK
    attn_scale = A_HEAD_DIM ** -0.5
    idx_scale = IDX_DIM ** -0.5

    def gather_rows(arr, idx):
        return arr[idx]

    def block(i):
        start = i * Q_BLOCK
        qb = lax.dynamic_slice_in_dim(q, start, Q_BLOCK, axis=1)
        qib = lax.dynamic_slice_in_dim(qi, start, Q_BLOCK, axis=1)
        wib = lax.dynamic_slice_in_dim(wi, start, Q_BLOCK, axis=1)
        tpos = start + jnp.arange(Q_BLOCK, dtype=jnp.int32)
        tchunk = tpos // CHUNK
        dots = jnp.einsum('bqhd,bsd->bqhs', qib, ki) * idx_scale
        iscore = jnp.einsum('bqhs,bqh->bqs', jax.nn.relu(dots), wib)
        admissible = key_chunk[None, :] <= tchunk[:, None]
        iscore = jnp.where(admissible[None], iscore, -jnp.inf)
        _, idx = lax.top_k(iscore, topk)
        kg = jax.vmap(gather_rows)(k, idx)
        vg = jax.vmap(gather_rows)(v, idx)
        logits = jnp.einsum('bqhd,bqkhd->bhqk', qb, kg).astype(jnp.float32) * attn_scale
        bias = rel_bias[t5_bucket(tpos[None, :, None] - idx)]
        logits = logits + jnp.transpose(bias, (0, 3, 1, 2)).astype(jnp.float32)
        valid = (idx // CHUNK) <= tchunk[None, :, None]
        logits = jnp.where(valid[:, None], logits, -jnp.inf)
        p = jax.nn.softmax(logits, axis=-1).astype(vg.dtype)
        return jnp.einsum('bhqk,bqkhd->bqhd', p, vg)

    out = lax.map(block, jnp.arange(n_blk, dtype=jnp.int32))
    out = jnp.moveaxis(out, 0, 1).reshape(bsz, seq, A_WIDTH)
    return (out * jax.nn.silu(z)) @ w_out


def sgu_mixer(h, w_in, ln_g, ln_b, w_s, b_s, w_out):
    bsz, seq, _ = h.shape
    u, v, z = jnp.split(h @ w_in, 3, axis=-1)
    v = layernorm(v, ln_g, ln_b)
    n_c = seq // SGU_CHUNK
    vc = v.reshape(bsz, n_c, SGU_CHUNK, B_GROUPS, B_GROUP_DIM)
    tri = jnp.tril(jnp.ones((SGU_CHUNK, SGU_CHUNK), dtype=w_s.dtype))
    ws = w_s * tri[None]
    mixed = jnp.einsum('gts,bcsge->bctge', ws, vc) + jnp.transpose(b_s)[None, None, :, :, None]
    y = u * mixed.reshape(bsz, seq, B_WIDTH)
    return (y * jax.nn.silu(z)) @ w_out


def setup_inputs(seed: int = 0) -> dict:
    key = jax.random.key(seed)
    ks = jax.random.split(key, 14)
    f32 = jnp.float32
    x = jax.random.normal(ks[0], (BATCH, SEQ, D_MODEL), f32)
    norm_g = 1.0 + 0.05 * jax.random.normal(ks[1], (DEPTH, D_MODEL), f32)
    final_g = 1.0 + 0.05 * jax.random.normal(ks[2], (D_MODEL,), f32)
    rel_bias = 0.5 * jax.random.normal(ks[3], (REL_BUCKETS, A_HEADS), f32)
    a_w_in = jax.random.normal(ks[4], (N_A_LAYERS, D_MODEL, A_IN_COLS), f32) * D_MODEL ** -0.5
    a_w_out = jax.random.normal(ks[5], (N_A_LAYERS, A_WIDTH, D_MODEL), f32) * A_WIDTH ** -0.5
    b_w_in = jax.random.normal(ks[6], (N_B_LAYERS, D_MODEL, B_IN_COLS), f32) * D_MODEL ** -0.5
    b_ln_g = 1.0 + 0.05 * jax.random.normal(ks[7], (N_B_LAYERS, B_WIDTH), f32)
    b_ln_b = 0.02 * jax.random.normal(ks[8], (N_B_LAYERS, B_WIDTH), f32)
    b_w_s = jax.random.normal(ks[9], (N_B_LAYERS, B_GROUPS, SGU_CHUNK, SGU_CHUNK), f32) * SGU_CHUNK ** -0.5
    b_b_s = 1.0 + 0.1 * jax.random.normal(ks[10], (N_B_LAYERS, B_GROUPS, SGU_CHUNK), f32)
    b_w_out = jax.random.normal(ks[11], (N_B_LAYERS, B_WIDTH, D_MODEL), f32) * B_WIDTH ** -0.5
    return {"x": x, "norm_g": norm_g, "final_g": final_g, "rel_bias": rel_bias,
            "a_w_in": a_w_in, "a_w_out": a_w_out,
            "b_w_in": b_w_in, "b_ln_g": b_ln_g, "b_ln_b": b_ln_b,
            "b_w_s": b_w_s, "b_b_s": b_b_s, "b_w_out": b_w_out}


def reference(x, norm_g, final_g, rel_bias, a_w_in, a_w_out,
              b_w_in, b_ln_g, b_ln_b, b_w_s, b_b_s, b_w_out):
    for i in range(DEPTH):
        h = rmsnorm(x, norm_g[i])
        j = i // N_MIXERS
        if i % N_MIXERS == 0:
            x = x + dsa_mixer(h, a_w_in[j], a_w_out[j], rel_bias)
        else:
            x = x + sgu_mixer(h, b_w_in[j], b_ln_g[j], b_ln_b[j],
                              b_w_s[j], b_b_s[j], b_w_out[j])
    return rmsnorm(x, final_g)
```

```python
import functools
import math

import numpy as np
import jax
import jax.numpy as jnp
from jax import lax
from jax.experimental import pallas as pl
from jax.experimental.pallas import tpu as pltpu

CHUNK = 64
RMS_EPS = 1e-6
LN_EPS = 1e-5

A_HEADS = 8
A_HEAD_DIM = 128
A_WIDTH = A_HEADS * A_HEAD_DIM
IDX_HEADS = 8
IDX_DIM = 64
TOPK_MAX = 256

REL_BUCKETS = 32
REL_MAX_DIST = 128

SGU_CHUNK = 128
B_GROUPS = 8

LANES = 128
KEY_TILE = 512
WORD_BITS = 32
VMEM_LIMIT = 56 * 1024 * 1024

INT_MIN = -2 ** 31
INT_MAX = 2 ** 31 - 1
NEG_BIG = -1e30

_BF16 = jnp.bfloat16
_F32 = jnp.float32
_I32 = jnp.int32


def _const_spec(shape):
    zeros = (0,) * len(shape)
    return pl.BlockSpec(shape, lambda *_: zeros, pipeline_mode=pl.Buffered(1))


def _rms_normalize(x, g):
    return x * lax.rsqrt(jnp.mean(x * x, axis=-1, keepdims=True) + RMS_EPS) * g


def _silu(z):
    return z / (1.0 + jnp.exp(-z))


def _proj_a_kernel(x_ref, g_ref, w_ref, qkv_ref, z_ref, qi_ref, tail_ref):
    hb = _rms_normalize(x_ref[...], g_ref[...]).astype(_BF16)

    def mm(lo, width):
        return jnp.dot(hb, w_ref[:, lo:lo + width], preferred_element_type=_F32)

    step = 512
    for c in range(0, 3 * A_WIDTH, step):
        qkv_ref[:, c:c + step] = mm(c, step).astype(_BF16)
    for c in range(0, A_WIDTH, step):
        z_ref[:, c:c + step] = mm(3 * A_WIDTH + c, step)
    qi_ref[...] = mm(4 * A_WIDTH, IDX_HEADS * IDX_DIM).astype(_BF16)
    tail_ref[...] = mm(4 * A_WIDTH + IDX_HEADS * IDX_DIM, LANES)


def _proj_a(x2, g, w_pad, tm):
    m, d = x2.shape
    n_qi = IDX_HEADS * IDX_DIM
    return pl.pallas_call(
        _proj_a_kernel,
        grid=(m // tm,),
        in_specs=[
            pl.BlockSpec((tm, d), lambda i: (i, 0)),
            _const_spec((1, d)),
            _const_spec(w_pad.shape),
        ],
        out_specs=[
            pl.BlockSpec((tm, 3 * A_WIDTH), lambda i: (i, 0)),
            pl.BlockSpec((tm, A_WIDTH), lambda i: (i, 0)),
            pl.BlockSpec((tm, n_qi), lambda i: (i, 0)),
            pl.BlockSpec((tm, LANES), lambda i: (i, 0)),
        ],
        out_shape=[
            jax.ShapeDtypeStruct((m, 3 * A_WIDTH), _BF16),
            jax.ShapeDtypeStruct((m, A_WIDTH), _F32),
            jax.ShapeDtypeStruct((m, n_qi), _BF16),
            jax.ShapeDtypeStruct((m, LANES), _F32),
        ],
        compiler_params=pltpu.CompilerParams(
            dimension_semantics=("arbitrary",), vmem_limit_bytes=VMEM_LIMIT),
        name="dsa_in_proj",
    )(x2, g, w_pad)


def _indexer_kernel(qi_ref, kit_ref, wi_ref, words_ref, keys_ref, *, tq, tk, topk, rows):
    i = pl.program_id(1)
    nt = lax.div(i * tq + tq + tk - 1, tk)
    wscaled = wi_ref[...] * (IDX_HEADS ** -0.5 * IDX_DIM ** -0.5)

    def score_keys(j):
        kt = kit_ref[:, pl.ds(pl.multiple_of(j * tk, tk), tk)]
        acc = jnp.zeros((tq, tk), _F32)
        for h in range(IDX_HEADS):
            d = jnp.dot(qi_ref[h], kt, preferred_element_type=_F32)
            acc = acc + jnp.maximum(d, 0.0) * wscaled[:, h:h + 1]
        bits = lax.bitcast_convert_type(acc, _I32)
        return bits ^ ((bits >> 31) & INT_MAX)

    def fill(j, carry):
        keys_ref[j] = score_keys(j)
        return carry

    lax.fori_loop(0, nt - 1, fill, 0)
    last = nt - 1
    t_chunk = (i * tq + lax.broadcasted_iota(_I32, (tq, tk), 0)) // CHUNK
    s_chunk = (last * tk + lax.broadcasted_iota(_I32, (tq, tk), 1)) // CHUNK
    keys_ref[last] = jnp.where(s_chunk <= t_chunk, score_keys(last), INT_MIN)

    lane = lax.broadcasted_iota(_I32, (rows, LANES), 1)
    kf = float(topk)

    def select_rows(rc, carry):
        r0 = pl.multiple_of(rc * rows, rows)

        def count(indicator):
            def body(j, acc):
                k = keys_ref[j, pl.ds(r0, rows), :]
                for c in range(tk // LANES):
                    acc = acc + indicator(k[:, c * LANES:(c + 1) * LANES], j, c)
                return acc
            acc = lax.fori_loop(0, nt, body, jnp.zeros((rows, LANES), _I32))
            tot = jnp.sum(acc.astype(_F32), axis=1, keepdims=True)
            return jnp.broadcast_to(tot, (rows, LANES))

        thr = jnp.full((rows, LANES), INT_MIN, _I32)
        thr = jnp.where(count(lambda k, j, c: jnp.where(k >= 0, 1, 0)) >= kf, 0, thr)

        def bit_step(b, thr):
            cand = thr | (jnp.int32(1) << (jnp.int32(30) - b))
            return jnp.where(count(lambda k, j, c: jnp.where(k >= cand, 1, 0)) >= kf, cand, thr)

        thr = lax.fori_loop(0, 31, bit_step, thr)
        thr = jnp.maximum(thr, INT_MIN + 1)

        n_ge = count(lambda k, j, c: jnp.where(k >= thr, 1, 0))
        n_gt = count(lambda k, j, c: jnp.where(k > thr, 1, 0))
        need = kf - n_gt
        straddle = n_ge > kf
        any_straddle = jnp.max(jnp.where(straddle, 1.0, 0.0)) > 0.0

        def pos(j, c):
            return j * tk + c * LANES + lane

        def tie_limit():
            nbits = max(1, int(math.ceil(math.log2(tk * WORD_BITS))))

            def step(b, p):
                cand = p | (jnp.int32(1) << (jnp.int32(nbits - 1) - b))
                before = count(lambda k, j, c: jnp.where(
                    k == thr, jnp.where(pos(j, c) < cand, 1, 0), 0))
                return jnp.where(before < need, cand, p)

            p = lax.fori_loop(0, nbits, step, jnp.zeros((rows, LANES), _I32))
            return jnp.where(straddle, p, INT_MAX)

        tie_pos = lax.cond(any_straddle, tie_limit,
                           lambda: jnp.full((rows, LANES), INT_MAX, _I32))

        def pack(j, w):
            k = keys_ref[j, pl.ds(r0, rows), :]
            bit = jnp.int32(1) << j
            parts = []
            for c in range(tk // LANES):
                kc = k[:, c * LANES:(c + 1) * LANES]
                tie = jnp.where(kc == thr, jnp.where(pos(j, c) <= tie_pos, bit, 0), 0)
                parts.append(jnp.where(kc > thr, bit, tie))
            return w | jnp.concatenate(parts, axis=1)

        words_ref[pl.ds(r0, rows), :] = lax.fori_loop(
            0, nt, pack, jnp.zeros((rows, tk), _I32))
        return carry

    lax.fori_loop(0, tq // rows, select_rows, 0)


def _indexer(qi_t, ki_t, wi, tq, topk):
    bsz, _, seq, _ = qi_t.shape
    tk = KEY_TILE
    assert seq % tk == 0 and seq // tk <= WORD_BITS and tq % CHUNK == 0 and seq % tq == 0
    kern = functools.partial(_indexer_kernel, tq=tq, tk=tk, topk=topk, rows=64)
    return pl.pallas_call(
        kern,
        grid=(bsz, seq // tq),
        in_specs=[
            pl.BlockSpec((None, IDX_HEADS, tq, IDX_DIM), lambda b, i: (b, 0, i, 0)),
            pl.BlockSpec((None, IDX_DIM, seq), lambda b, i: (b, 0, 0)),
            pl.BlockSpec((None, tq, IDX_HEADS), lambda b, i: (b, i, 0)),
        ],
        out_specs=pl.BlockSpec((None, tq, tk), lambda b, i: (b, i, 0)),
        out_shape=jax.ShapeDtypeStruct((bsz, seq, tk), _I32),
        scratch_shapes=[pltpu.VMEM((seq // tk, tq, tk), _I32)],
        compiler_params=pltpu.CompilerParams(
            dimension_semantics=("arbitrary", "arbitrary"), vmem_limit_bytes=VMEM_LIMIT),
        name="dsa_indexer_topk",
    )(qi_t, ki_t, wi)


def _attn_kernel(bt_ref, it_ref, jt_ref, q_ref, k_ref, v_ref, words_ref, t0_ref, t1_ref,
                 z_ref, x_ref, wo_ref, o_ref, m_sc, l_sc, acc_sc, s_sc, *, tq, tk):
    step = pl.program_id(0)
    i = it_ref[step]
    j = jt_ref[step]
    scale = A_HEAD_DIM ** -0.5
    nsub = tq // LANES

    @pl.when(j == 0)
    def _():
        m_sc[...] = jnp.full(m_sc.shape, NEG_BIG, _F32)
        l_sc[...] = jnp.zeros(l_sc.shape, _F32)
        acc_sc[...] = jnp.zeros(acc_sc.shape, _F32)

    selected = ((words_ref[...] >> j) & 1) != 0

    for h in range(A_HEADS):
        hs = slice(h * A_HEAD_DIM, (h + 1) * A_HEAD_DIM)
        s_sc[...] = lax.dot_general(
            q_ref[:, hs], k_ref[:, hs], (((1,), (1,)), ((), ())),
            preferred_element_type=_F32) * scale

        @pl.when(j == i)
        def _():
            for a in range(nsub):
                rs = slice(a * LANES, (a + 1) * LANES)
                s_sc[rs, rs] += t0_ref[h]
                if a > 0:
                    s_sc[rs, (a - 1) * LANES:a * LANES] += t1_ref[h]

        @pl.when(j == i - 1)
        def _():
            s_sc[0:LANES, tk - LANES:tk] += t1_ref[h]

        s = jnp.where(selected, s_sc[...], -jnp.inf)
        m_prev = m_sc[h]
        m_new = jnp.maximum(m_prev, jnp.max(s, axis=1, keepdims=True))
        alpha = jnp.exp(m_prev - m_new)
        p = jnp.exp(s - m_new[:, 0:1])
        l_sc[h] = alpha * l_sc[h] + jnp.sum(p, axis=1, keepdims=True)
        acc_sc[:, hs] = alpha * acc_sc[:, hs] + jnp.dot(
            p.astype(_BF16), v_ref[:, hs], preferred_element_type=_F32)
        m_sc[h] = m_new

    @pl.when(j == i)
    def _():
        parts = []
        for h in range(A_HEADS):
            hs = slice(h * A_HEAD_DIM, (h + 1) * A_HEAD_DIM)
            parts.append(((acc_sc[:, hs] / l_sc[h]) * _silu(z_ref[:, hs])).astype(_BF16))
        gated = jnp.concatenate(parts, axis=1)
        o_ref[...] = x_ref[...] + jnp.dot(gated, wo_ref[...], preferred_element_type=_F32)


def _attention(qkv, words, t0, t1, z, x, wo, tq):
    bsz, seq, d = x.shape
    tk = KEY_TILE
    assert tq == tk and seq % tq == 0
    nq = seq // tq
    bt, it, jt = [], [], []
    for b in range(bsz):
        for i in range(nq):
            for j in range(i + 1):
                bt.append(b)
                it.append(i)
                jt.append(j)
    tables = [jnp.asarray(np.asarray(t, np.int32)) for t in (bt, it, jt)]

    grid_spec = pltpu.PrefetchScalarGridSpec(
        num_scalar_prefetch=3,
        grid=(len(bt),),
        in_specs=[
            pl.BlockSpec((None, tq, A_WIDTH), lambda s, b, i, j: (b[s], i[s], 0)),
            pl.BlockSpec((None, tk, A_WIDTH), lambda s, b, i, j: (b[s], j[s], 1)),
            pl.BlockSpec((None, tk, A_WIDTH), lambda s, b, i, j: (b[s], j[s], 2)),
            pl.BlockSpec((None, tq, tk), lambda s, b, i, j: (b[s], i[s], 0)),
            _const_spec(t0.shape),
            _const_spec(t1.shape),
            pl.BlockSpec((None, tq, A_WIDTH), lambda s, b, i, j: (b[s], i[s], 0)),
            pl.BlockSpec((None, tq, d), lambda s, b, i, j: (b[s], i[s], 0)),
            _const_spec(wo.shape),
        ],
        out_specs=pl.BlockSpec((None, tq, d), lambda s, b, i, j: (b[s], i[s], 0)),
        scratch_shapes=[
            pltpu.VMEM((A_HEADS, tq, LANES), _F32),
            pltpu.VMEM((A_HEADS, tq, LANES), _F32),
            pltpu.VMEM((tq, A_WIDTH), _F32),
            pltpu.VMEM((tq, tk), _F32),
        ],
    )
    return pl.pallas_call(
        functools.partial(_attn_kernel, tq=tq, tk=tk),
        grid_spec=grid_spec,
        out_shape=jax.ShapeDtypeStruct((bsz, seq, d), _F32),
        compiler_params=pltpu.CompilerParams(
            dimension_semantics=("arbitrary",), vmem_limit_bytes=VMEM_LIMIT),
        name="dsa_attention",
    )(*tables, qkv, qkv, qkv, words, t0, t1, z, x, wo)


def _sgu_kernel(x_ref, g_ref, win_ref, lng_ref, lnb_ref, ws_ref, bst_ref, wout_ref, fg_ref,
                o_ref, *, tm, width):
    x = x_ref[...]
    hb = _rms_normalize(x, g_ref[...]).astype(_BF16)
    gdim = width // B_GROUPS

    def mm(lo):
        return jnp.dot(hb, win_ref[:, lo:lo + width], preferred_element_type=_F32)

    v = mm(width)
    mu = jnp.mean(v, axis=-1, keepdims=True)
    vc = v - mu
    var = jnp.mean(vc * vc, axis=-1, keepdims=True)
    vn = ((vc * lax.rsqrt(var + LN_EPS)) * lng_ref[...] + lnb_ref[...]).astype(_BF16)

    row = lax.broadcasted_iota(_I32, (SGU_CHUNK, SGU_CHUNK), 0)
    col = lax.broadcasted_iota(_I32, (SGU_CHUNK, SGU_CHUNK), 1)
    bst = bst_ref[...]
    cols = []
    for g in range(B_GROUPS):
        ws = jnp.where(row >= col, ws_ref[g], 0.0).astype(_BF16)
        bias = bst[:, g:g + 1]
        chunks = []
        for c in range(tm // SGU_CHUNK):
            blk = vn[c * SGU_CHUNK:(c + 1) * SGU_CHUNK, g * gdim:(g + 1) * gdim]
            chunks.append(jnp.dot(ws, blk, preferred_element_type=_F32) + bias)
        cols.append(jnp.concatenate(chunks, axis=0))
    mixed = jnp.concatenate(cols, axis=1)

    y = (mm(0) * mixed * _silu(mm(2 * width))).astype(_BF16)
    x2 = x + jnp.dot(y, wout_ref[...], preferred_element_type=_F32)
    o_ref[...] = _rms_normalize(x2, fg_ref[...])


def _sgu_layer(x2, g, w_in, ln_g, ln_b, w_s, b_s_t, w_out, final_g, tm):
    m, d = x2.shape
    width = w_out.shape[0]
    return pl.pallas_call(
        functools.partial(_sgu_kernel, tm=tm, width=width),
        grid=(m // tm,),
        in_specs=[
            pl.BlockSpec((tm, d), lambda i: (i, 0)),
            _const_spec((1, d)),
            _const_spec(w_in.shape),
            _const_spec((1, width)),
            _const_spec((1, width)),
            _const_spec(w_s.shape),
            _const_spec(b_s_t.shape),
            _const_spec(w_out.shape),
            _const_spec((1, d)),
        ],
        out_specs=pl.BlockSpec((tm, d), lambda i: (i, 0)),
        out_shape=jax.ShapeDtypeStruct((m, d), _F32),
        compiler_params=pltpu.CompilerParams(
            dimension_semantics=("arbitrary",), vmem_limit_bytes=VMEM_LIMIT),
        name="sgu_layer_final_norm",
    )(x2, g, w_in, ln_g, ln_b, w_s, b_s_t, w_out, final_g)


def _t5_bucket(rel):
    half = REL_BUCKETS // 2
    max_exact = half // 2
    ret = jnp.where(rel < 0, half, 0)
    n = jnp.abs(rel)
    nf = jnp.maximum(n, 1).astype(jnp.float32)
    large = max_exact + (jnp.log(nf / max_exact) / math.log(REL_MAX_DIST / max_exact)
                         * (half - max_exact)).astype(jnp.int32)
    large = jnp.minimum(large, half - 1)
    return ret + jnp.where(n < max_exact, n, large)


def _bias_tables(rel_bias):
    a = jnp.arange(LANES, dtype=jnp.int32)
    rel0 = a[:, None] - a[None, :]
    far = rel_bias[_t5_bucket(jnp.int32(4 * REL_MAX_DIST))]
    t0 = jnp.transpose(rel_bias[_t5_bucket(rel0)] - far, (2, 0, 1))
    t1 = jnp.transpose(rel_bias[_t5_bucket(rel0 + LANES)] - far, (2, 0, 1))
    return t0, t1


def kernel(x, norm_g, final_g, rel_bias, a_w_in, a_w_out, b_w_in, b_ln_g, b_ln_b, b_w_s,
           b_b_s, b_w_out):
    bsz, seq, d = x.shape
    assert norm_g.shape[0] == 2 and a_w_in.shape[0] == 1 and b_w_in.shape[0] == 1
    m = bsz * seq
    n_qi = IDX_HEADS * IDX_DIM
    topk = min(TOPK_MAX, seq // 4)

    cols = a_w_in.shape[2]
    w_pad = jnp.pad(a_w_in[0], ((0, 0), (0, 4 * A_WIDTH + n_qi + LANES - cols))).astype(_BF16)
    qkv, z, qi, tail = _proj_a(x.reshape(m, d), norm_g[0:1], w_pad, tm=512)
    qi_t = jnp.transpose(qi.reshape(bsz, seq, IDX_HEADS, IDX_DIM), (0, 2, 1, 3))
    ki_t = jnp.transpose(tail[:, :IDX_DIM].reshape(bsz, seq, IDX_DIM), (0, 2, 1)).astype(_BF16)
    wi = tail[:, IDX_DIM:IDX_DIM + IDX_HEADS].reshape(bsz, seq, IDX_HEADS)
    words = _indexer(qi_t, ki_t, wi, tq=256, topk=topk)
    t0, t1 = _bias_tables(rel_bias)
    x1 = _attention(qkv.reshape(bsz, seq, 3 * A_WIDTH), words, t0, t1,
                    z.reshape(bsz, seq, A_WIDTH), x, a_w_out[0].astype(_BF16), tq=KEY_TILE)

    out = _sgu_layer(x1.reshape(m, d), norm_g[1:2], b_w_in[0].astype(_BF16),
                     b_ln_g[0:1], b_ln_b[0:1], b_w_s[0], jnp.transpose(b_b_s[0]),
                     b_w_out[0].astype(_BF16), final_g.reshape(1, d), tm=256)
    return out.reshape(bsz, seq, d)
```

```python
import functools
import math

import numpy as np
import jax
import jax.numpy as jnp
from jax import lax
from jax.experimental import pallas as pl
from jax.experimental.pallas import tpu as pltpu

CHUNK = 64
RMS_EPS = 1e-6
LN_EPS = 1e-5

A_HEADS = 8
A_HEAD_DIM = 128
A_WIDTH = A_HEADS * A_HEAD_DIM
IDX_HEADS = 8
IDX_DIM = 64
TOPK_MAX = 256

REL_BUCKETS = 32
REL_MAX_DIST = 128

SGU_CHUNK = 128
B_GROUPS = 8

LANES = 128
KEY_TILE = 512
WORD_BITS = 32
VMEM_LIMIT = 56 * 1024 * 1024

LOG2_E = math.log2(math.e)
Q_SCALE = A_HEAD_DIM ** -0.5 * LOG2_E

INT_MIN = -2 ** 31
INT_MAX = 2 ** 31 - 1
NEG_BIG = -1e30

_BF16 = jnp.bfloat16
_F32 = jnp.float32
_I32 = jnp.int32


def _const_spec(shape):
    zeros = (0,) * len(shape)
    return pl.BlockSpec(shape, lambda *_: zeros, pipeline_mode=pl.Buffered(1))


def _rms_normalize(x, g):
    return x * lax.rsqrt(jnp.mean(x * x, axis=-1, keepdims=True) + RMS_EPS) * g


def _silu(z):
    return z / (1.0 + jnp.exp(-z))


def _proj_a_kernel(x_ref, g_ref, w_ref, qkv_ref, z_ref, qi_ref, tail_ref):
    hb = _rms_normalize(x_ref[...], g_ref[...]).astype(_BF16)

    def mm(lo, width):
        return jnp.dot(hb, w_ref[:, lo:lo + width], preferred_element_type=_F32)

    step = 512
    for c in range(0, 3 * A_WIDTH, step):
        scale = Q_SCALE if c < A_WIDTH else 1.0
        qkv_ref[:, c:c + step] = (mm(c, step) * scale).astype(_BF16)
    for c in range(0, A_WIDTH, step):
        z_ref[:, c:c + step] = mm(3 * A_WIDTH + c, step)
    qi_ref[...] = mm(4 * A_WIDTH, IDX_HEADS * IDX_DIM).astype(_BF16)
    tail_ref[...] = mm(4 * A_WIDTH + IDX_HEADS * IDX_DIM, LANES)


def _proj_a(x2, g, w_pad, tm):
    m, d = x2.shape
    n_qi = IDX_HEADS * IDX_DIM
    return pl.pallas_call(
        _proj_a_kernel,
        grid=(m // tm,),
        in_specs=[
            pl.BlockSpec((tm, d), lambda i: (i, 0)),
            _const_spec((1, d)),
            _const_spec(w_pad.shape),
        ],
        out_specs=[
            pl.BlockSpec((tm, 3 * A_WIDTH), lambda i: (i, 0)),
            pl.BlockSpec((tm, A_WIDTH), lambda i: (i, 0)),
            pl.BlockSpec((tm, n_qi), lambda i: (i, 0)),
            pl.BlockSpec((tm, LANES), lambda i: (i, 0)),
        ],
        out_shape=[
            jax.ShapeDtypeStruct((m, 3 * A_WIDTH), _BF16),
            jax.ShapeDtypeStruct((m, A_WIDTH), _F32),
            jax.ShapeDtypeStruct((m, n_qi), _BF16),
            jax.ShapeDtypeStruct((m, LANES), _F32),
        ],
        compiler_params=pltpu.CompilerParams(
            dimension_semantics=("arbitrary",), vmem_limit_bytes=VMEM_LIMIT),
        name="dsa_in_proj",
    )(x2, g, w_pad)


def _indexer_kernel(qi_ref, kit_ref, wi_ref, words_ref, keys_ref, *, tq, tk, topk, rows):
    i = pl.program_id(1)
    nt = lax.div(i * tq + tq + tk - 1, tk)
    wscaled = wi_ref[...] * (IDX_HEADS ** -0.5 * IDX_DIM ** -0.5)

    def score_keys(j):
        kt = kit_ref[:, pl.ds(pl.multiple_of(j * tk, tk), tk)]
        acc = jnp.zeros((tq, tk), _F32)
        for h in range(IDX_HEADS):
            d = jnp.dot(qi_ref[h], kt, preferred_element_type=_F32)
            acc = acc + jnp.maximum(d, 0.0) * wscaled[:, h:h + 1]
        bits = lax.bitcast_convert_type(acc, _I32)
        return bits ^ ((bits >> 31) & INT_MAX)

    def fill(j, carry):
        keys_ref[j] = score_keys(j)
        return carry

    lax.fori_loop(0, nt - 1, fill, 0)
    last = nt - 1
    t_chunk = (i * tq + lax.broadcasted_iota(_I32, (tq, tk), 0)) // CHUNK
    s_chunk = (last * tk + lax.broadcasted_iota(_I32, (tq, tk), 1)) // CHUNK
    keys_ref[last] = jnp.where(s_chunk <= t_chunk, score_keys(last), INT_MIN)

    lane = lax.broadcasted_iota(_I32, (rows, LANES), 1)
    kf = float(topk)

    def select_rows(rc, carry):
        r0 = pl.multiple_of(rc * rows, rows)

        def count(indicator):
            def body(j, acc):
                k = keys_ref[j, pl.ds(r0, rows), :]
                for c in range(tk // LANES):
                    acc = acc + indicator(k[:, c * LANES:(c + 1) * LANES], j, c)
                return acc
            acc = lax.fori_loop(0, nt, body, jnp.zeros((rows, LANES), _I32))
            tot = jnp.sum(acc.astype(_F32), axis=1, keepdims=True)
            return jnp.broadcast_to(tot, (rows, LANES))

        thr = jnp.full((rows, LANES), INT_MIN, _I32)
        thr = jnp.where(count(lambda k, j, c: jnp.where(k >= 0, 1, 0)) >= kf, 0, thr)

        def bit_step(b, thr):
            cand = thr | (jnp.int32(1) << (jnp.int32(30) - b))
            return jnp.where(count(lambda k, j, c: jnp.where(k >= cand, 1, 0)) >= kf, cand, thr)

        thr = lax.fori_loop(0, 31, bit_step, thr)
        thr = jnp.maximum(thr, INT_MIN + 1)

        n_ge = count(lambda k, j, c: jnp.where(k >= thr, 1, 0))
        n_gt = count(lambda k, j, c: jnp.where(k > thr, 1, 0))
        need = kf - n_gt
        straddle = n_ge > kf
        any_straddle = jnp.max(jnp.where(straddle, 1.0, 0.0)) > 0.0

        def pos(j, c):
            return j * tk + c * LANES + lane

        def tie_limit():
            nbits = max(1, int(math.ceil(math.log2(tk * WORD_BITS))))

            def step(b, p):
                cand = p | (jnp.int32(1) << (jnp.int32(nbits - 1) - b))
                before = count(lambda k, j, c: jnp.where(
                    k == thr, jnp.where(pos(j, c) < cand, 1, 0), 0))
                return jnp.where(before < need, cand, p)

            p = lax.fori_loop(0, nbits, step, jnp.zeros((rows, LANES), _I32))
            return jnp.where(straddle, p, INT_MAX)

        tie_pos = lax.cond(any_straddle, tie_limit,
                           lambda: jnp.full((rows, LANES), INT_MAX, _I32))

        def pack(j, w):
            k = keys_ref[j, pl.ds(r0, rows), :]
            bit = jnp.int32(1) << j
            parts = []
            for c in range(tk // LANES):
                kc = k[:, c * LANES:(c + 1) * LANES]
                tie = jnp.where(kc == thr, jnp.where(pos(j, c) <= tie_pos, bit, 0), 0)
                parts.append(jnp.where(kc > thr, bit, tie))
            return w | jnp.concatenate(parts, axis=1)

        words_ref[pl.ds(r0, rows), :] = lax.fori_loop(
            0, nt, pack, jnp.zeros((rows, tk), _I32))
        return carry

    lax.fori_loop(0, tq // rows, select_rows, 0)


def _indexer(qi_t, ki_t, wi, tq, topk):
    bsz, _, seq, _ = qi_t.shape
    tk = KEY_TILE
    assert seq % tk == 0 and seq // tk <= WORD_BITS and tq % CHUNK == 0 and seq % tq == 0
    kern = functools.partial(_indexer_kernel, tq=tq, tk=tk, topk=topk, rows=64)
    return pl.pallas_call(
        kern,
        grid=(bsz, seq // tq),
        in_specs=[
            pl.BlockSpec((None, IDX_HEADS, tq, IDX_DIM), lambda b, i: (b, 0, i, 0)),
            pl.BlockSpec((None, IDX_DIM, seq), lambda b, i: (b, 0, 0)),
            pl.BlockSpec((None, tq, IDX_HEADS), lambda b, i: (b, i, 0)),
        ],
        out_specs=pl.BlockSpec((None, tq, tk), lambda b, i: (b, i, 0)),
        out_shape=jax.ShapeDtypeStruct((bsz, seq, tk), _I32),
        scratch_shapes=[pltpu.VMEM((seq // tk, tq, tk), _I32)],
        compiler_params=pltpu.CompilerParams(
            dimension_semantics=("arbitrary", "arbitrary"), vmem_limit_bytes=VMEM_LIMIT),
        name="dsa_indexer_topk",
    )(qi_t, ki_t, wi)


def _attn_kernel(bt_ref, it_ref, jt_ref, q_ref, k_ref, vt_ref, words_ref, t0_ref, t1_ref,
                 z_ref, x_ref, wo_ref, o_ref, m_sc, l_sc, acc_sc, mask_sc, s_sc, *, tq, tk):
    step = pl.program_id(0)
    i = it_ref[step]
    j = jt_ref[step]
    nsub = tq // LANES

    @pl.when(j == 0)
    def _():
        m_sc[...] = jnp.full(m_sc.shape, NEG_BIG, _F32)
        l_sc[...] = jnp.zeros(l_sc.shape, _F32)
        acc_sc[...] = jnp.zeros(acc_sc.shape, _F32)

    mask_sc[...] = jnp.where(((words_ref[...] >> j) & 1) != 0, 0.0, -jnp.inf)

    def head(h, near):
        hs = slice(h * A_HEAD_DIM, (h + 1) * A_HEAD_DIM)
        s = lax.dot_general(k_ref[:, hs], q_ref[:, hs], (((1,), (1,)), ((), ())),
                            preferred_element_type=_F32)
        if near:
            s_sc[...] = s

            @pl.when(j == i)
            def _():
                for a in range(nsub):
                    cs = slice(a * LANES, (a + 1) * LANES)
                    s_sc[cs, cs] += t0_ref[h]
                    if a > 0:
                        s_sc[(a - 1) * LANES:a * LANES, cs] += t1_ref[h]

            @pl.when(j == i - 1)
            def _():
                s_sc[tk - LANES:tk, 0:LANES] += t1_ref[h]

            s = s_sc[...]
        s = s + mask_sc[...]
        m_prev = m_sc[h:h + 1, :]
        m_new = jnp.maximum(m_prev, jnp.max(s, axis=0, keepdims=True))
        alpha = jnp.exp2(m_prev - m_new)
        p = jnp.exp2(s - m_new)
        l_sc[h:h + 1, :] = alpha * l_sc[h:h + 1, :] + jnp.sum(p, axis=0, keepdims=True)
        acc_sc[hs, :] = alpha * acc_sc[hs, :] + jnp.dot(
            vt_ref[hs, :], p.astype(_BF16), preferred_element_type=_F32)
        m_sc[h:h + 1, :] = m_new

    @pl.when(j < i - 1)
    def _():
        for h in range(A_HEADS):
            head(h, near=False)

    @pl.when(j >= i - 1)
    def _():
        for h in range(A_HEADS):
            head(h, near=True)

    @pl.when(j == i)
    def _():
        parts = []
        for h in range(A_HEADS):
            hs = slice(h * A_HEAD_DIM, (h + 1) * A_HEAD_DIM)
            out_h = jnp.transpose(acc_sc[hs, :] / l_sc[h:h + 1, :])
            parts.append((out_h * _silu(z_ref[:, hs])).astype(_BF16))
        gated = jnp.concatenate(parts, axis=1)
        o_ref[...] = x_ref[...] + jnp.dot(gated, wo_ref[...], preferred_element_type=_F32)


def _attention(qkv, v_t, words_t, t0, t1, z, x, wo, tq):
    bsz, seq, d = x.shape
    tk = KEY_TILE
    assert tq == tk and seq % tq == 0
    nq = seq // tq
    bt, it, jt = [], [], []
    for b in range(bsz):
        for i in range(nq):
            for j in range(i + 1):
                bt.append(b)
                it.append(i)
                jt.append(j)
    tables = [jnp.asarray(np.asarray(t, np.int32)) for t in (bt, it, jt)]

    grid_spec = pltpu.PrefetchScalarGridSpec(
        num_scalar_prefetch=3,
        grid=(len(bt),),
        in_specs=[
            pl.BlockSpec((None, tq, A_WIDTH), lambda s, b, i, j: (b[s], i[s], 0)),
            pl.BlockSpec((None, tk, A_WIDTH), lambda s, b, i, j: (b[s], j[s], 1)),
            pl.BlockSpec((None, A_WIDTH, tk), lambda s, b, i, j: (b[s], 0, j[s])),
            pl.BlockSpec((None, tk, tq), lambda s, b, i, j: (b[s], 0, i[s])),
            _const_spec(t0.shape),
            _const_spec(t1.shape),
            pl.BlockSpec((None, tq, A_WIDTH), lambda s, b, i, j: (b[s], i[s], 0)),
            pl.BlockSpec((None, tq, d), lambda s, b, i, j: (b[s], i[s], 0)),
            _const_spec(wo.shape),
        ],
        out_specs=pl.BlockSpec((None, tq, d), lambda s, b, i, j: (b[s], i[s], 0)),
        scratch_shapes=[
            pltpu.VMEM((A_HEADS, tq), _F32),
            pltpu.VMEM((A_HEADS, tq), _F32),
            pltpu.VMEM((A_WIDTH, tq), _F32),
            pltpu.VMEM((tk, tq), _F32),
            pltpu.VMEM((tk, tq), _F32),
        ],
    )
    return pl.pallas_call(
        functools.partial(_attn_kernel, tq=tq, tk=tk),
        grid_spec=grid_spec,
        out_shape=jax.ShapeDtypeStruct((bsz, seq, d), _F32),
        compiler_params=pltpu.CompilerParams(
            dimension_semantics=("arbitrary",), vmem_limit_bytes=VMEM_LIMIT),
        name="dsa_attention",
    )(*tables, qkv, qkv, v_t, words_t, t0, t1, z, x, wo)


def _sgu_kernel(x_ref, g_ref, win_ref, lng_ref, lnb_ref, ws_ref, bst_ref, wout_ref, fg_ref,
                o_ref, *, tm, width):
    x = x_ref[...]
    hb = _rms_normalize(x, g_ref[...]).astype(_BF16)
    gdim = width // B_GROUPS

    def mm(lo):
        return jnp.dot(hb, win_ref[:, lo:lo + width], preferred_element_type=_F32)

    v = mm(width)
    mu = jnp.mean(v, axis=-1, keepdims=True)
    vc = v - mu
    var = jnp.mean(vc * vc, axis=-1, keepdims=True)
    vn = ((vc * lax.rsqrt(var + LN_EPS)) * lng_ref[...] + lnb_ref[...]).astype(_BF16)

    row = lax.broadcasted_iota(_I32, (SGU_CHUNK, SGU_CHUNK), 0)
    col = lax.broadcasted_iota(_I32, (SGU_CHUNK, SGU_CHUNK), 1)
    bst = bst_ref[...]
    cols = []
    for g in range(B_GROUPS):
        ws = jnp.where(row >= col, ws_ref[g], 0.0).astype(_BF16)
        bias = bst[:, g:g + 1]
        chunks = []
        for c in range(tm // SGU_CHUNK):
            blk = vn[c * SGU_CHUNK:(c + 1) * SGU_CHUNK, g * gdim:(g + 1) * gdim]
            chunks.append(jnp.dot(ws, blk, preferred_element_type=_F32) + bias)
        cols.append(jnp.concatenate(chunks, axis=0))
    mixed = jnp.concatenate(cols, axis=1)

    y = (mm(0) * mixed * _silu(mm(2 * width))).astype(_BF16)
    x2 = x + jnp.dot(y, wout_ref[...], preferred_element_type=_F32)
    o_ref[...] = _rms_normalize(x2, fg_ref[...])


def _sgu_layer(x2, g, w_in, ln_g, ln_b, w_s, b_s_t, w_out, final_g, tm):
    m, d = x2.shape
    width = w_out.shape[0]
    return pl.pallas_call(
        functools.partial(_sgu_kernel, tm=tm, width=width),
        grid=(m // tm,),
        in_specs=[
            pl.BlockSpec((tm, d), lambda i: (i, 0)),
            _const_spec((1, d)),
            _const_spec(w_in.shape),
            _const_spec((1, width)),
            _const_spec((1, width)),
            _const_spec(w_s.shape),
            _const_spec(b_s_t.shape),
            _const_spec(w_out.shape),
            _const_spec((1, d)),
        ],
        out_specs=pl.BlockSpec((tm, d), lambda i: (i, 0)),
        out_shape=jax.ShapeDtypeStruct((m, d), _F32),
        compiler_params=pltpu.CompilerParams(
            dimension_semantics=("arbitrary",), vmem_limit_bytes=VMEM_LIMIT),
        name="sgu_layer_final_norm",
    )(x2, g, w_in, ln_g, ln_b, w_s, b_s_t, w_out, final_g)


def _t5_bucket(rel):
    half = REL_BUCKETS // 2
    max_exact = half // 2
    ret = jnp.where(rel < 0, half, 0)
    n = jnp.abs(rel)
    nf = jnp.maximum(n, 1).astype(jnp.float32)
    large = max_exact + (jnp.log(nf / max_exact) / math.log(REL_MAX_DIST / max_exact)
                         * (half - max_exact)).astype(jnp.int32)
    large = jnp.minimum(large, half - 1)
    return ret + jnp.where(n < max_exact, n, large)


def _bias_tables(rel_bias):
    a = jnp.arange(LANES, dtype=jnp.int32)
    rel0 = a[None, :] - a[:, None]
    far = rel_bias[_t5_bucket(jnp.int32(4 * REL_MAX_DIST))]
    t0 = jnp.transpose(rel_bias[_t5_bucket(rel0)] - far, (2, 0, 1)) * LOG2_E
    t1 = jnp.transpose(rel_bias[_t5_bucket(rel0 + LANES)] - far, (2, 0, 1)) * LOG2_E
    return t0, t1


def kernel(x, norm_g, final_g, rel_bias, a_w_in, a_w_out, b_w_in, b_ln_g, b_ln_b, b_w_s,
           b_b_s, b_w_out):
    bsz, seq, d = x.shape
    assert norm_g.shape[0] == 2 and a_w_in.shape[0] == 1 and b_w_in.shape[0] == 1
    m = bsz * seq
    n_qi = IDX_HEADS * IDX_DIM
    topk = min(TOPK_MAX, seq // 4)

    cols = a_w_in.shape[2]
    w_pad = jnp.pad(a_w_in[0], ((0, 0), (0, 4 * A_WIDTH + n_qi + LANES - cols))).astype(_BF16)
    qkv, z, qi, tail = _proj_a(x.reshape(m, d), norm_g[0:1], w_pad, tm=512)
    qi_t = jnp.transpose(qi.reshape(bsz, seq, IDX_HEADS, IDX_DIM), (0, 2, 1, 3))
    ki_t = jnp.transpose(tail[:, :IDX_DIM].reshape(bsz, seq, IDX_DIM), (0, 2, 1)).astype(_BF16)
    wi = tail[:, IDX_DIM:IDX_DIM + IDX_HEADS].reshape(bsz, seq, IDX_HEADS)
    words = _indexer(qi_t, ki_t, wi, tq=256, topk=topk)
    t0, t1 = _bias_tables(rel_bias)
    qkv = qkv.reshape(bsz, seq, 3 * A_WIDTH)
    v_t = jnp.transpose(qkv[:, :, 2 * A_WIDTH:], (0, 2, 1))
    words_t = jnp.transpose(words, (0, 2, 1))
    x1 = _attention(qkv, v_t, words_t, t0, t1,
                    z.reshape(bsz, seq, A_WIDTH), x, a_w_out[0].astype(_BF16), tq=KEY_TILE)

    out = _sgu_layer(x1.reshape(m, d), norm_g[1:2], b_w_in[0].astype(_BF16),
                     b_ln_g[0:1], b_ln_b[0:1], b_w_s[0], jnp.transpose(b_b_s[0]),
                     b_w_out[0].astype(_BF16), final_g.reshape(1, d), tm=256)
    return out.reshape(bsz, seq, d)
```

```python
import functools
import math

import numpy as np
import jax
import jax.numpy as jnp
from jax import lax
from jax.experimental import pallas as pl
from jax.experimental.pallas import tpu as pltpu

CHUNK = 64
RMS_EPS = 1e-6
LN_EPS = 1e-5

A_HEADS = 8
A_HEAD_DIM = 128
A_WIDTH = A_HEADS * A_HEAD_DIM
IDX_HEADS = 8
IDX_DIM = 64
TOPK_MAX = 256

REL_BUCKETS = 32
REL_MAX_DIST = 128

SGU_CHUNK = 128
B_GROUPS = 8

LANES = 128
KEY_TILE = 512
WORD_BITS = 32
VMEM_LIMIT = 56 * 1024 * 1024

LOG2_E = math.log2(math.e)
Q_SCALE = A_HEAD_DIM ** -0.5 * LOG2_E

INT_MIN = -2 ** 31
INT_MAX = 2 ** 31 - 1
NEG_BIG = -1e30

_BF16 = jnp.bfloat16
_F32 = jnp.float32
_I32 = jnp.int32


def _const_spec(shape):
    zeros = (0,) * len(shape)
    return pl.BlockSpec(shape, lambda *_: zeros, pipeline_mode=pl.Buffered(1))


def _rms_normalize(x, g):
    return x * lax.rsqrt(jnp.mean(x * x, axis=-1, keepdims=True) + RMS_EPS) * g


def _silu(z):
    return z / (1.0 + jnp.exp(-z))


def _proj_a_kernel(x_ref, g_ref, w_ref, qkv_ref, z_ref, qi_ref, tail_ref):
    hb = _rms_normalize(x_ref[...], g_ref[...]).astype(_BF16)

    def mm(lo, width):
        return jnp.dot(hb, w_ref[:, lo:lo + width], preferred_element_type=_F32)

    step = 512
    for c in range(0, 3 * A_WIDTH, step):
        scale = Q_SCALE if c < A_WIDTH else 1.0
        qkv_ref[:, c:c + step] = (mm(c, step) * scale).astype(_BF16)
    for c in range(0, A_WIDTH, step):
        z_ref[:, c:c + step] = mm(3 * A_WIDTH + c, step)
    qi_ref[...] = mm(4 * A_WIDTH, IDX_HEADS * IDX_DIM).astype(_BF16)
    tail_ref[...] = mm(4 * A_WIDTH + IDX_HEADS * IDX_DIM, LANES)


def _proj_a(x2, g, w_pad, tm):
    m, d = x2.shape
    n_qi = IDX_HEADS * IDX_DIM
    return pl.pallas_call(
        _proj_a_kernel,
        grid=(m // tm,),
        in_specs=[
            pl.BlockSpec((tm, d), lambda i: (i, 0)),
            _const_spec((1, d)),
            _const_spec(w_pad.shape),
        ],
        out_specs=[
            pl.BlockSpec((tm, 3 * A_WIDTH), lambda i: (i, 0)),
            pl.BlockSpec((tm, A_WIDTH), lambda i: (i, 0)),
            pl.BlockSpec((tm, n_qi), lambda i: (i, 0)),
            pl.BlockSpec((tm, LANES), lambda i: (i, 0)),
        ],
        out_shape=[
            jax.ShapeDtypeStruct((m, 3 * A_WIDTH), _BF16),
            jax.ShapeDtypeStruct((m, A_WIDTH), _F32),
            jax.ShapeDtypeStruct((m, n_qi), _BF16),
            jax.ShapeDtypeStruct((m, LANES), _F32),
        ],
        compiler_params=pltpu.CompilerParams(
            dimension_semantics=("arbitrary",), vmem_limit_bytes=VMEM_LIMIT),
        name="dsa_in_proj",
    )(x2, g, w_pad)


def _bit_transpose(vregs):
    a = list(reversed(vregs))
    j, mask = 16, 0x0000FFFF
    while j:
        k = 0
        while k < WORD_BITS:
            t = (a[k] ^ lax.shift_right_logical(a[k + j], jnp.int32(j))) & mask
            a[k] = a[k] ^ t
            a[k + j] = a[k + j] ^ (t << j)
            k = (k + j + 1) & ~j
        j >>= 1
        mask = (mask ^ (mask << j)) & 0xFFFFFFFF
        if mask >= 2 ** 31:
            mask -= 2 ** 32
    return list(reversed(a))


def _indexer_kernel(ki_ref, qit_ref, wit_ref, out_ref, p_sc, c_sc, g_sc, *, tq, topk):
    i = pl.program_id(1)
    nlg = tq // LANES
    half = KEY_TILE // 2
    nj = i + 1
    wscaled = wit_ref[...] * (IDX_HEADS ** -0.5 * IDX_DIM ** -0.5)

    def planes_of_tile(j, diagonal):
        kt = ki_ref[pl.ds(pl.multiple_of(j * KEY_TILE, KEY_TILE), KEY_TILE), :]
        acc = jnp.zeros((KEY_TILE, tq), _F32)
        for h in range(IDX_HEADS):
            d = jnp.dot(kt, qit_ref[h], preferred_element_type=_F32)
            acc = acc + jnp.maximum(d, 0.0) * wscaled[h:h + 1, :]
        bits = lax.bitcast_convert_type(acc, _I32)
        u = bits ^ ((bits >> 31) | INT_MIN)
        if diagonal:
            s_chunk = lax.broadcasted_iota(_I32, (KEY_TILE, tq), 0) // CHUNK
            t_chunk = lax.broadcasted_iota(_I32, (KEY_TILE, tq), 1) // CHUNK
            u = jnp.where(s_chunk <= t_chunk, u, 0)
        for hf in range(2):
            for lg in range(nlg):
                blk = [u[hf * half + 8 * m:hf * half + 8 * m + 8, lg * LANES:(lg + 1) * LANES]
                       for m in range(WORD_BITS)]
                planes = _bit_transpose(blk)
                for b in range(WORD_BITS):
                    p_sc[j, hf, lg, 8 * b:8 * b + 8, :] = planes[b]

    def fill(j, carry):
        planes_of_tile(j, diagonal=False)
        return carry

    lax.fori_loop(0, i, fill, 0)
    planes_of_tile(i, diagonal=True)

    def init(j, carry):
        c_sc[j] = jnp.full(c_sc.shape[1:], -1, _I32)
        g_sc[j] = jnp.zeros(g_sc.shape[1:], _I32)
        return carry

    lax.fori_loop(0, nj, init, 0)

    def per_query_sum(parts):
        return [jnp.broadcast_to(jnp.sum(p, axis=0, keepdims=True), (8, LANES)) for p in parts]

    def count(word_fn):
        def body(j, acc):
            acc = list(acc)
            for hf in range(2):
                for lg in range(nlg):
                    acc[lg] = acc[lg] + lax.population_count(word_fn(j, hf, lg))
            return tuple(acc)
        zero = jnp.zeros((8, LANES), _I32)
        return per_query_sum(lax.fori_loop(0, nj, body, (zero,) * nlg))

    def radix_step(step, carry):
        need, found = carry
        b8 = pl.multiple_of((WORD_BITS - 1 - step) * 8, 8)

        def plane(j, hf, lg):
            return p_sc[j, hf, lg, pl.ds(b8, 8), :]

        ones = count(lambda j, hf, lg: c_sc[j, hf, lg] & plane(j, hf, lg))
        take = [ones[lg] >= need[lg] for lg in range(nlg)]

        def update(j, c):
            for hf in range(2):
                for lg in range(nlg):
                    cand = c_sc[j, hf, lg]
                    hit = cand & plane(j, hf, lg)
                    c_sc[j, hf, lg] = jnp.where(take[lg], hit, cand ^ hit)
                    g_sc[j, hf, lg] = g_sc[j, hf, lg] | jnp.where(take[lg], 0, hit)
            return c

        lax.fori_loop(0, nj, update, 0)
        need = tuple(jnp.where(take[lg], need[lg], need[lg] - ones[lg]) for lg in range(nlg))
        found = tuple(jnp.where(take[lg], 1, found[lg]) for lg in range(nlg))
        return need, found

    need0 = (jnp.full((8, LANES), topk, _I32),) * nlg
    found0 = (jnp.zeros((8, LANES), _I32),) * nlg
    need, found = lax.fori_loop(0, WORD_BITS, radix_step, (need0, found0))
    found = [f != 0 for f in found]

    n_ties = count(lambda j, hf, lg: c_sc[j, hf, lg])
    straddle = [found[lg] & (n_ties[lg] > need[lg]) for lg in range(nlg)]
    any_straddle = functools.reduce(
        jnp.maximum, [jnp.max(jnp.where(s, 1.0, 0.0)) for s in straddle]) > 0.0
    sub = lax.broadcasted_iota(_I32, (8, LANES), 0)

    def upto(limit, j, hf):
        n = jnp.clip(((limit - (j * KEY_TILE + hf * half + sub)) >> 3) + 1, 0, WORD_BITS)
        return jnp.where(n >= WORD_BITS, -1, (jnp.int32(1) << jnp.minimum(n, WORD_BITS - 1)) - 1)

    def tie_limit():
        nbits = max(1, int(math.ceil(math.log2(p_sc.shape[0] * KEY_TILE))))

        def step(t, limit):
            cand = [limit[lg] | (jnp.int32(1) << (jnp.int32(nbits - 1) - t)) for lg in range(nlg)]
            before = count(lambda j, hf, lg: c_sc[j, hf, lg] & upto(cand[lg] - 1, j, hf))
            return tuple(jnp.where(before[lg] < need[lg], cand[lg], limit[lg])
                         for lg in range(nlg))

        limit = lax.fori_loop(0, nbits, step, (jnp.zeros((8, LANES), _I32),) * nlg)
        return tuple(jnp.where(straddle[lg], limit[lg], INT_MAX) for lg in range(nlg))

    limit = lax.cond(any_straddle, tie_limit,
                     lambda: (jnp.full((8, LANES), INT_MAX, _I32),) * nlg)

    def emit(j, carry):
        for hf in range(2):
            for lg in range(nlg):
                ties = jnp.where(found[lg], c_sc[j, hf, lg] & upto(limit[lg], j, hf), 0)
                out_ref[lg, j, hf] = g_sc[j, hf, lg] | ties
        return carry

    lax.fori_loop(0, nj, emit, 0)

    def clear(j, carry):
        for lg in range(nlg):
            out_ref[lg, j] = jnp.zeros((2, 8, LANES), _I32)
        return carry

    lax.fori_loop(nj, p_sc.shape[0], clear, 0)


def _indexer(ki, qi_t, wi_t, tq, topk):
    bsz, seq, _ = ki.shape
    nt = seq // KEY_TILE
    nlg = tq // LANES
    assert seq % KEY_TILE == 0 and tq == KEY_TILE and KEY_TILE % (2 * 8 * WORD_BITS) == 0
    return pl.pallas_call(
        functools.partial(_indexer_kernel, tq=tq, topk=topk),
        grid=(bsz, seq // tq),
        in_specs=[
            pl.BlockSpec((None, seq, IDX_DIM), lambda b, i: (b, 0, 0),
                         pipeline_mode=pl.Buffered(1)),
            pl.BlockSpec((None, IDX_HEADS, IDX_DIM, tq), lambda b, i: (b, 0, 0, i)),
            pl.BlockSpec((None, IDX_HEADS, tq), lambda b, i: (b, 0, i)),
        ],
        out_specs=pl.BlockSpec((None, nlg, nt, 2, 8, LANES), lambda b, i: (b, i, 0, 0, 0, 0)),
        out_shape=jax.ShapeDtypeStruct((bsz, seq // LANES, nt, 2, 8, LANES), _I32),
        scratch_shapes=[
            pltpu.VMEM((nt, 2, nlg, KEY_TILE // 2, LANES), _I32),
            pltpu.VMEM((nt, 2, nlg, 8, LANES), _I32),
            pltpu.VMEM((nt, 2, nlg, 8, LANES), _I32),
        ],
        compiler_params=pltpu.CompilerParams(
            dimension_semantics=("arbitrary", "arbitrary"), vmem_limit_bytes=VMEM_LIMIT),
        name="dsa_indexer_topk",
    )(ki, qi_t, wi_t)


def _attn_kernel(bt_ref, it_ref, jt_ref, q_ref, k_ref, vt_ref, words_ref, t0_ref, t1_ref,
                 z_ref, x_ref, wo_ref, o_ref, m_sc, l_sc, acc_sc, mask_sc, s_sc, *, tq, tk):
    step = pl.program_id(0)
    i = it_ref[step]
    j = jt_ref[step]
    nsub = tq // LANES

    @pl.when(j == 0)
    def _():
        m_sc[...] = jnp.full(m_sc.shape, NEG_BIG, _F32)
        l_sc[...] = jnp.zeros(l_sc.shape, _F32)
        acc_sc[...] = jnp.zeros(acc_sc.shape, _F32)

    for hf in range(2):
        for lg in range(tq // LANES):
            w = words_ref[lg, hf]
            for m in range(WORD_BITS):
                r0 = hf * (tk // 2) + 8 * m
                mask_sc[r0:r0 + 8, lg * LANES:(lg + 1) * LANES] = jnp.where(
                    (w << (WORD_BITS - 1 - m)) < 0, 0.0, -jnp.inf)

    def head(h, near):
        hs = slice(h * A_HEAD_DIM, (h + 1) * A_HEAD_DIM)
        s = lax.dot_general(k_ref[:, hs], q_ref[:, hs], (((1,), (1,)), ((), ())),
                            preferred_element_type=_F32)
        if near:
            s_sc[...] = s

            @pl.when(j == i)
            def _():
                for a in range(nsub):
                    cs = slice(a * LANES, (a + 1) * LANES)
                    s_sc[cs, cs] += t0_ref[h]
                    if a > 0:
                        s_sc[(a - 1) * LANES:a * LANES, cs] += t1_ref[h]

            @pl.when(j == i - 1)
            def _():
                s_sc[tk - LANES:tk, 0:LANES] += t1_ref[h]

            s = s_sc[...]
        s = s + mask_sc[...]
        m_prev = m_sc[h:h + 1, :]
        m_new = jnp.maximum(m_prev, jnp.max(s, axis=0, keepdims=True))
        alpha = jnp.exp2(m_prev - m_new)
        p = jnp.exp2(s - m_new)
        l_sc[h:h + 1, :] = alpha * l_sc[h:h + 1, :] + jnp.sum(p, axis=0, keepdims=True)
        acc_sc[hs, :] = alpha * acc_sc[hs, :] + jnp.dot(
            vt_ref[hs, :], p.astype(_BF16), preferred_element_type=_F32)
        m_sc[h:h + 1, :] = m_new

    @pl.when(j < i - 1)
    def _():
        for h in range(A_HEADS):
            head(h, near=False)

    @pl.when(j >= i - 1)
    def _():
        for h in range(A_HEADS):
            head(h, near=True)

    @pl.when(j == i)
    def _():
        parts = []
        for h in range(A_HEADS):
            hs = slice(h * A_HEAD_DIM, (h + 1) * A_HEAD_DIM)
            out_h = jnp.transpose(acc_sc[hs, :] / l_sc[h:h + 1, :])
            parts.append((out_h * _silu(z_ref[:, hs])).astype(_BF16))
        gated = jnp.concatenate(parts, axis=1)
        o_ref[...] = x_ref[...] + jnp.dot(gated, wo_ref[...], preferred_element_type=_F32)


def _attention(qkv, v_t, words_t, t0, t1, z, x, wo, tq):
    bsz, seq, d = x.shape
    tk = KEY_TILE
    assert tq == tk and seq % tq == 0
    nq = seq // tq
    bt, it, jt = [], [], []
    for b in range(bsz):
        for i in range(nq):
            for j in range(i + 1):
                bt.append(b)
                it.append(i)
                jt.append(j)
    tables = [jnp.asarray(np.asarray(t, np.int32)) for t in (bt, it, jt)]

    grid_spec = pltpu.PrefetchScalarGridSpec(
        num_scalar_prefetch=3,
        grid=(len(bt),),
        in_specs=[
            pl.BlockSpec((None, tq, A_WIDTH), lambda s, b, i, j: (b[s], i[s], 0)),
            pl.BlockSpec((None, tk, A_WIDTH), lambda s, b, i, j: (b[s], j[s], 1)),
            pl.BlockSpec((None, A_WIDTH, tk), lambda s, b, i, j: (b[s], 0, j[s])),
            pl.BlockSpec((None, tq // LANES, None, 2, 8, LANES),
                         lambda s, b, i, j: (b[s], i[s], j[s], 0, 0, 0)),
            _const_spec(t0.shape),
            _const_spec(t1.shape),
            pl.BlockSpec((None, tq, A_WIDTH), lambda s, b, i, j: (b[s], i[s], 0)),
            pl.BlockSpec((None, tq, d), lambda s, b, i, j: (b[s], i[s], 0)),
            _const_spec(wo.shape),
        ],
        out_specs=pl.BlockSpec((None, tq, d), lambda s, b, i, j: (b[s], i[s], 0)),
        scratch_shapes=[
            pltpu.VMEM((A_HEADS, tq), _F32),
            pltpu.VMEM((A_HEADS, tq), _F32),
            pltpu.VMEM((A_WIDTH, tq), _F32),
            pltpu.VMEM((tk, tq), _F32),
            pltpu.VMEM((tk, tq), _F32),
        ],
    )
    return pl.pallas_call(
        functools.partial(_attn_kernel, tq=tq, tk=tk),
        grid_spec=grid_spec,
        out_shape=jax.ShapeDtypeStruct((bsz, seq, d), _F32),
        compiler_params=pltpu.CompilerParams(
            dimension_semantics=("arbitrary",), vmem_limit_bytes=VMEM_LIMIT),
        name="dsa_attention",
    )(*tables, qkv, qkv, v_t, words_t, t0, t1, z, x, wo)


def _sgu_kernel(x_ref, g_ref, win_ref, lng_ref, lnb_ref, ws_ref, bst_ref, wout_ref, fg_ref,
                o_ref, *, tm, width):
    x = x_ref[...]
    hb = _rms_normalize(x, g_ref[...]).astype(_BF16)
    gdim = width // B_GROUPS

    def mm(lo):
        return jnp.dot(hb, win_ref[:, lo:lo + width], preferred_element_type=_F32)

    v = mm(width)
    mu = jnp.mean(v, axis=-1, keepdims=True)
    vc = v - mu
    var = jnp.mean(vc * vc, axis=-1, keepdims=True)
    vn = ((vc * lax.rsqrt(var + LN_EPS)) * lng_ref[...] + lnb_ref[...]).astype(_BF16)

    row = lax.broadcasted_iota(_I32, (SGU_CHUNK, SGU_CHUNK), 0)
    col = lax.broadcasted_iota(_I32, (SGU_CHUNK, SGU_CHUNK), 1)
    bst = bst_ref[...]
    cols = []
    for g in range(B_GROUPS):
        ws = jnp.where(row >= col, ws_ref[g], 0.0).astype(_BF16)
        bias = bst[:, g:g + 1]
        chunks = []
        for c in range(tm // SGU_CHUNK):
            blk = vn[c * SGU_CHUNK:(c + 1) * SGU_CHUNK, g * gdim:(g + 1) * gdim]
            chunks.append(jnp.dot(ws, blk, preferred_element_type=_F32) + bias)
        cols.append(jnp.concatenate(chunks, axis=0))
    mixed = jnp.concatenate(cols, axis=1)

    y = (mm(0) * mixed * _silu(mm(2 * width))).astype(_BF16)
    x2 = x + jnp.dot(y, wout_ref[...], preferred_element_type=_F32)
    o_ref[...] = _rms_normalize(x2, fg_ref[...])


def _sgu_layer(x2, g, w_in, ln_g, ln_b, w_s, b_s_t, w_out, final_g, tm):
    m, d = x2.shape
    width = w_out.shape[0]
    return pl.pallas_call(
        functools.partial(_sgu_kernel, tm=tm, width=width),
        grid=(m // tm,),
        in_specs=[
            pl.BlockSpec((tm, d), lambda i: (i, 0)),
            _const_spec((1, d)),
            _const_spec(w_in.shape),
            _const_spec((1, width)),
            _const_spec((1, width)),
            _const_spec(w_s.shape),
            _const_spec(b_s_t.shape),
            _const_spec(w_out.shape),
            _const_spec((1, d)),
        ],
        out_specs=pl.BlockSpec((tm, d), lambda i: (i, 0)),
        out_shape=jax.ShapeDtypeStruct((m, d), _F32),
        compiler_params=pltpu.CompilerParams(
            dimension_semantics=("arbitrary",), vmem_limit_bytes=VMEM_LIMIT),
        name="sgu_layer_final_norm",
    )(x2, g, w_in, ln_g, ln_b, w_s, b_s_t, w_out, final_g)


def _t5_bucket(rel):
    half = REL_BUCKETS // 2
    max_exact = half // 2
    ret = jnp.where(rel < 0, half, 0)
    n = jnp.abs(rel)
    nf = jnp.maximum(n, 1).astype(jnp.float32)
    large = max_exact + (jnp.log(nf / max_exact) / math.log(REL_MAX_DIST / max_exact)
                         * (half - max_exact)).astype(jnp.int32)
    large = jnp.minimum(large, half - 1)
    return ret + jnp.where(n < max_exact, n, large)


def _bias_tables(rel_bias):
    a = jnp.arange(LANES, dtype=jnp.int32)
    rel0 = a[None, :] - a[:, None]
    far = rel_bias[_t5_bucket(jnp.int32(4 * REL_MAX_DIST))]
    t0 = jnp.transpose(rel_bias[_t5_bucket(rel0)] - far, (2, 0, 1)) * LOG2_E
    t1 = jnp.transpose(rel_bias[_t5_bucket(rel0 + LANES)] - far, (2, 0, 1)) * LOG2_E
    return t0, t1


def kernel(x, norm_g, final_g, rel_bias, a_w_in, a_w_out, b_w_in, b_ln_g, b_ln_b, b_w_s,
           b_b_s, b_w_out):
    bsz, seq, d = x.shape
    assert norm_g.shape[0] == 2 and a_w_in.shape[0] == 1 and b_w_in.shape[0] == 1
    m = bsz * seq
    n_qi = IDX_HEADS * IDX_DIM
    topk = min(TOPK_MAX, seq // 4)

    cols = a_w_in.shape[2]
    w_pad = jnp.pad(a_w_in[0], ((0, 0), (0, 4 * A_WIDTH + n_qi + LANES - cols))).astype(_BF16)
    qkv, z, qi, tail = _proj_a(x.reshape(m, d), norm_g[0:1], w_pad, tm=512)
    qi_t = jnp.transpose(qi.reshape(bsz, seq, IDX_HEADS, IDX_DIM), (0, 2, 3, 1))
    ki = tail[:, :IDX_DIM].reshape(bsz, seq, IDX_DIM).astype(_BF16)
    wi_t = jnp.transpose(tail[:, IDX_DIM:IDX_DIM + IDX_HEADS].reshape(bsz, seq, IDX_HEADS),
                         (0, 2, 1))
    words = _indexer(ki, qi_t, wi_t, tq=KEY_TILE, topk=topk)
    t0, t1 = _bias_tables(rel_bias)
    qkv = qkv.reshape(bsz, seq, 3 * A_WIDTH)
    v_t = jnp.transpose(qkv[:, :, 2 * A_WIDTH:], (0, 2, 1))
    x1 = _attention(qkv, v_t, words, t0, t1,
                    z.reshape(bsz, seq, A_WIDTH), x, a_w_out[0].astype(_BF16), tq=KEY_TILE)

    out = _sgu_layer(x1.reshape(m, d), norm_g[1:2], b_w_in[0].astype(_BF16),
                     b_ln_g[0:1], b_ln_b[0:1], b_w_s[0], jnp.transpose(b_b_s[0]),
                     b_w_out[0].astype(_BF16), final_g.reshape(1, d), tm=256)
    return out.reshape(bsz, seq, d)
```

```python
import functools
import math

import numpy as np
import jax
import jax.numpy as jnp
from jax import lax
from jax.experimental import pallas as pl
from jax.experimental.pallas import tpu as pltpu

CHUNK = 64
RMS_EPS = 1e-6
LN_EPS = 1e-5

A_HEADS = 8
A_HEAD_DIM = 128
A_WIDTH = A_HEADS * A_HEAD_DIM
IDX_HEADS = 8
IDX_DIM = 64
TOPK_MAX = 256

REL_BUCKETS = 32
REL_MAX_DIST = 128

SGU_CHUNK = 128
B_GROUPS = 8

LANES = 128
KEY_TILE = 512
WORD_BITS = 32
STAGE_PAD = 16
VMEM_LIMIT = 56 * 1024 * 1024

LOG2_E = math.log2(math.e)
Q_SCALE = A_HEAD_DIM ** -0.5 * LOG2_E

INT_MIN = -2 ** 31
INT_MAX = 2 ** 31 - 1
NEG_BIG = -1e30

_BF16 = jnp.bfloat16
_F32 = jnp.float32
_I32 = jnp.int32


def _const_spec(shape):
    zeros = (0,) * len(shape)
    return pl.BlockSpec(shape, lambda *_: zeros, pipeline_mode=pl.Buffered(1))


def _rms_normalize(x, g):
    return x * lax.rsqrt(jnp.mean(x * x, axis=-1, keepdims=True) + RMS_EPS) * g


def _silu(z):
    return z / (1.0 + jnp.exp(-z))


def _proj_a_kernel(x_ref, g_ref, w_ref, qkv_ref, z_ref, qi_ref, tail_ref):
    hb = _rms_normalize(x_ref[...], g_ref[...]).astype(_BF16)

    def mm(lo, width):
        return jnp.dot(hb, w_ref[:, lo:lo + width], preferred_element_type=_F32)

    step = 512
    for c in range(0, 3 * A_WIDTH, step):
        scale = Q_SCALE if c < A_WIDTH else 1.0
        qkv_ref[:, c:c + step] = (mm(c, step) * scale).astype(_BF16)
    for c in range(0, A_WIDTH, step):
        z_ref[:, c:c + step] = mm(3 * A_WIDTH + c, step)
    qi_ref[...] = mm(4 * A_WIDTH, IDX_HEADS * IDX_DIM).astype(_BF16)
    tail_ref[...] = mm(4 * A_WIDTH + IDX_HEADS * IDX_DIM, LANES)


def _proj_a(x2, g, w_pad, tm):
    m, d = x2.shape
    n_qi = IDX_HEADS * IDX_DIM
    return pl.pallas_call(
        _proj_a_kernel,
        grid=(m // tm,),
        in_specs=[
            pl.BlockSpec((tm, d), lambda i: (i, 0)),
            _const_spec((1, d)),
            _const_spec(w_pad.shape),
        ],
        out_specs=[
            pl.BlockSpec((tm, 3 * A_WIDTH), lambda i: (i, 0)),
            pl.BlockSpec((tm, A_WIDTH), lambda i: (i, 0)),
            pl.BlockSpec((tm, n_qi), lambda i: (i, 0)),
            pl.BlockSpec((tm, LANES), lambda i: (i, 0)),
        ],
        out_shape=[
            jax.ShapeDtypeStruct((m, 3 * A_WIDTH), _BF16),
            jax.ShapeDtypeStruct((m, A_WIDTH), _F32),
            jax.ShapeDtypeStruct((m, n_qi), _BF16),
            jax.ShapeDtypeStruct((m, LANES), _F32),
        ],
        compiler_params=pltpu.CompilerParams(
            dimension_semantics=("arbitrary",), vmem_limit_bytes=VMEM_LIMIT),
        name="dsa_in_proj",
    )(x2, g, w_pad)


def _bit_transpose(vregs):
    a = list(reversed(vregs))
    j, mask = 16, 0x0000FFFF
    while j:
        k = 0
        while k < WORD_BITS:
            t = (a[k] ^ lax.shift_right_logical(a[k + j], jnp.int32(j))) & mask
            a[k] = a[k] ^ t
            a[k + j] = a[k + j] ^ (t << j)
            k = (k + j + 1) & ~j
        j >>= 1
        mask = (mask ^ (mask << j)) & 0xFFFFFFFF
        if mask >= 2 ** 31:
            mask -= 2 ** 32
    return list(reversed(a))


def _indexer_kernel(ki_ref, qit_ref, wit_ref, out_ref, p_sc, c_sc, g_sc, *, tq, topk):
    i = pl.program_id(1)
    nlg = tq // LANES
    half = KEY_TILE // 2
    nj = i + 1
    wscaled = wit_ref[...] * (IDX_HEADS ** -0.5 * IDX_DIM ** -0.5)

    def planes_of_tile(j, diagonal):
        kt = ki_ref[pl.ds(pl.multiple_of(j * KEY_TILE, KEY_TILE), KEY_TILE), :]
        acc = jnp.zeros((KEY_TILE, tq), _F32)
        for h in range(IDX_HEADS):
            d = jnp.dot(kt, qit_ref[h], preferred_element_type=_F32)
            acc = acc + jnp.maximum(d, 0.0) * wscaled[h:h + 1, :]
        bits = lax.bitcast_convert_type(acc, _I32)
        u = bits ^ ((bits >> 31) | INT_MIN)
        if diagonal:
            s_chunk = lax.broadcasted_iota(_I32, (KEY_TILE, tq), 0) // CHUNK
            t_chunk = lax.broadcasted_iota(_I32, (KEY_TILE, tq), 1) // CHUNK
            u = jnp.where(s_chunk <= t_chunk, u, 0)
        for hf in range(2):
            for lg in range(nlg):
                blk = [u[hf * half + 8 * m:hf * half + 8 * m + 8, lg * LANES:(lg + 1) * LANES]
                       for m in range(WORD_BITS)]
                planes = _bit_transpose(blk)
                for b in range(WORD_BITS):
                    p_sc[j, hf, lg, 8 * b:8 * b + 8, :] = planes[b]

    def fill(j, carry):
        planes_of_tile(j, diagonal=False)
        return carry

    lax.fori_loop(0, i, fill, 0)
    planes_of_tile(i, diagonal=True)

    def init(j, carry):
        c_sc[j] = jnp.full(c_sc.shape[1:], -1, _I32)
        g_sc[j] = jnp.zeros(g_sc.shape[1:], _I32)
        return carry

    lax.fori_loop(0, nj, init, 0)

    def per_query_sum(parts):
        return [jnp.broadcast_to(jnp.sum(p, axis=0, keepdims=True), (8, LANES)) for p in parts]

    def count(word_fn):
        def body(j, acc):
            acc = list(acc)
            for hf in range(2):
                for lg in range(nlg):
                    acc[lg] = acc[lg] + lax.population_count(word_fn(j, hf, lg))
            return tuple(acc)
        zero = jnp.zeros((8, LANES), _I32)
        return per_query_sum(lax.fori_loop(0, nj, body, (zero,) * nlg))

    def radix_step(step, carry):
        need, found = carry
        b8 = pl.multiple_of((WORD_BITS - 1 - step) * 8, 8)

        def plane(j, hf, lg):
            return p_sc[j, hf, lg, pl.ds(b8, 8), :]

        ones = count(lambda j, hf, lg: c_sc[j, hf, lg] & plane(j, hf, lg))
        take = [ones[lg] >= need[lg] for lg in range(nlg)]

        def update(j, c):
            for hf in range(2):
                for lg in range(nlg):
                    cand = c_sc[j, hf, lg]
                    hit = cand & plane(j, hf, lg)
                    c_sc[j, hf, lg] = jnp.where(take[lg], hit, cand ^ hit)
                    g_sc[j, hf, lg] = g_sc[j, hf, lg] | jnp.where(take[lg], 0, hit)
            return c

        lax.fori_loop(0, nj, update, 0)
        need = tuple(jnp.where(take[lg], need[lg], need[lg] - ones[lg]) for lg in range(nlg))
        found = tuple(jnp.where(take[lg], 1, found[lg]) for lg in range(nlg))
        return need, found

    need0 = (jnp.full((8, LANES), topk, _I32),) * nlg
    found0 = (jnp.zeros((8, LANES), _I32),) * nlg
    need, found = lax.fori_loop(0, WORD_BITS, radix_step, (need0, found0))
    found = [f != 0 for f in found]

    n_ties = count(lambda j, hf, lg: c_sc[j, hf, lg])
    straddle = [found[lg] & (n_ties[lg] > need[lg]) for lg in range(nlg)]
    any_straddle = functools.reduce(
        jnp.maximum, [jnp.max(jnp.where(s, 1.0, 0.0)) for s in straddle]) > 0.0
    sub = lax.broadcasted_iota(_I32, (8, LANES), 0)

    def upto(limit, j, hf):
        n = jnp.clip(((limit - (j * KEY_TILE + hf * half + sub)) >> 3) + 1, 0, WORD_BITS)
        return jnp.where(n >= WORD_BITS, -1, (jnp.int32(1) << jnp.minimum(n, WORD_BITS - 1)) - 1)

    def tie_limit():
        nbits = max(1, int(math.ceil(math.log2(p_sc.shape[0] * KEY_TILE))))

        def step(t, limit):
            cand = [limit[lg] | (jnp.int32(1) << (jnp.int32(nbits - 1) - t)) for lg in range(nlg)]
            before = count(lambda j, hf, lg: c_sc[j, hf, lg] & upto(cand[lg] - 1, j, hf))
            return tuple(jnp.where(before[lg] < need[lg], cand[lg], limit[lg])
                         for lg in range(nlg))

        limit = lax.fori_loop(0, nbits, step, (jnp.zeros((8, LANES), _I32),) * nlg)
        return tuple(jnp.where(straddle[lg], limit[lg], INT_MAX) for lg in range(nlg))

    limit = lax.cond(any_straddle, tie_limit,
                     lambda: (jnp.full((8, LANES), INT_MAX, _I32),) * nlg)

    def emit(j, carry):
        for hf in range(2):
            for lg in range(nlg):
                ties = jnp.where(found[lg], c_sc[j, hf, lg] & upto(limit[lg], j, hf), 0)
                out_ref[lg, j, hf] = g_sc[j, hf, lg] | ties
        return carry

    lax.fori_loop(0, nj, emit, 0)

    def clear(j, carry):
        for lg in range(nlg):
            out_ref[lg, j] = jnp.zeros((2, 8, LANES), _I32)
        return carry

    lax.fori_loop(nj, p_sc.shape[0], clear, 0)


def _indexer(ki, qi_t, wi_t, tq, topk):
    bsz, seq, _ = ki.shape
    nt = seq // KEY_TILE
    nlg = tq // LANES
    assert seq % KEY_TILE == 0 and tq == KEY_TILE and KEY_TILE % (2 * 8 * WORD_BITS) == 0
    return pl.pallas_call(
        functools.partial(_indexer_kernel, tq=tq, topk=topk),
        grid=(bsz, seq // tq),
        in_specs=[
            pl.BlockSpec((None, seq, IDX_DIM), lambda b, i: (b, 0, 0),
                         pipeline_mode=pl.Buffered(1)),
            pl.BlockSpec((None, IDX_HEADS, IDX_DIM, tq), lambda b, i: (b, 0, 0, i)),
            pl.BlockSpec((None, IDX_HEADS, tq), lambda b, i: (b, 0, i)),
        ],
        out_specs=pl.BlockSpec((None, nlg, nt, 2, 8, LANES), lambda b, i: (b, i, 0, 0, 0, 0)),
        out_shape=jax.ShapeDtypeStruct((bsz, seq // LANES, nt, 2, 8, LANES), _I32),
        scratch_shapes=[
            pltpu.VMEM((nt, 2, nlg, KEY_TILE // 2, LANES), _I32),
            pltpu.VMEM((nt, 2, nlg, 8, LANES), _I32),
            pltpu.VMEM((nt, 2, nlg, 8, LANES), _I32),
        ],
        compiler_params=pltpu.CompilerParams(
            dimension_semantics=("arbitrary", "arbitrary"), vmem_limit_bytes=VMEM_LIMIT),
        name="dsa_indexer_topk",
    )(ki, qi_t, wi_t)


def _attn_kernel(bt_ref, it_ref, jt_ref, q_ref, k_ref, vt_ref, words_ref, t0_ref, t1_ref,
                 z_ref, x_ref, wo_ref, o_ref, m_sc, l_sc, acc_sc, mask_sc, s0_sc, s1_sc, p0_sc, p1_sc,
                 *, tq, tk):
    step = pl.program_id(0)
    i = it_ref[step]
    j = jt_ref[step]
    nsub = tq // LANES

    @pl.when(j == 0)
    def _():
        m_sc[...] = jnp.full(m_sc.shape, NEG_BIG, _F32)
        l_sc[...] = jnp.zeros(l_sc.shape, _F32)
        acc_sc[...] = jnp.zeros(acc_sc.shape, _F32)

    def build_mask():
        for hf in range(2):
            for lg in range(tq // LANES):
                w = words_ref[lg, hf]
                for m in range(WORD_BITS):
                    r0 = hf * (tk // 2) + 8 * m
                    mask_sc[r0:r0 + 8, lg * LANES:(lg + 1) * LANES] = jnp.where(
                        (w << (WORD_BITS - 1 - m)) < 0, 0.0, -jnp.inf)

    rows = 64
    s_bufs = (s0_sc, s1_sc)
    p_bufs = (p0_sc, p1_sc)
    off = pl.multiple_of(jnp.minimum(j, 0) * STAGE_PAD, STAGE_PAD)

    def fold_rows(x, op):
        return op(x.reshape(x.shape[0] // 8, 8, tq), axis=0)

    def bias_block(h, kind, key_blk, qry_blk, r):
        if kind == "diag" and key_blk == qry_blk:
            return t0_ref[h, r:r + rows, :]
        if kind == "diag" and key_blk + 1 == qry_blk:
            return t1_ref[h, r:r + rows, :]
        if kind == "sub" and key_blk == nsub - 1 and qry_blk == 0:
            return t1_ref[h, r:r + rows, :]
        return None

    def logits(h, kind):
        hs = slice(h * A_HEAD_DIM, (h + 1) * A_HEAD_DIM)
        s_buf = s_bufs[h % 2]
        s = lax.dot_general(k_ref[:, hs], q_ref[:, hs], (((1,), (1,)), ((), ())),
                            preferred_element_type=_F32)
        m8 = None
        for r0 in range(0, tk, rows):
            t = s[r0:r0 + rows, :] + mask_sc[r0:r0 + rows, :]
            if kind != "far":
                parts = []
                for c in range(nsub):
                    blk = t[:, c * LANES:(c + 1) * LANES]
                    bias = bias_block(h, kind, r0 // LANES, c, r0 % LANES)
                    parts.append(blk if bias is None else blk + bias)
                t = jnp.concatenate(parts, axis=1)
            s_buf[pl.ds(off + r0, rows), :] = t
            m8 = fold_rows(t, jnp.max) if m8 is None else jnp.maximum(m8, fold_rows(t, jnp.max))
        return m8

    def accumulate(h, m8):
        hs = slice(h * A_HEAD_DIM, (h + 1) * A_HEAD_DIM)
        s_buf, p_buf = s_bufs[h % 2], p_bufs[h % 2]
        m_prev = m_sc[h:h + 1, :]
        m_new = jnp.maximum(m_prev, jnp.max(m8, axis=0, keepdims=True))
        alpha = jnp.exp2(m_prev - m_new)
        for r0 in range(0, tk, rows):
            p = jnp.exp2(s_buf[pl.ds(off + r0, rows), :] - m_new)
            p_buf[pl.ds(off + r0, rows), :] = p.astype(_BF16)
        pv = jnp.dot(vt_ref[h], p_buf[pl.ds(off, tk), :], preferred_element_type=_F32)
        acc_sc[hs, :] = alpha * acc_sc[hs, :] + pv[0:A_HEAD_DIM, :]
        l_sc[h:h + 1, :] = alpha * l_sc[h:h + 1, :] + pv[A_HEAD_DIM:A_HEAD_DIM + 1, :]
        m_sc[h:h + 1, :] = m_new

    def all_heads(kind):
        build_mask()
        m8 = logits(0, kind)
        for h in range(A_HEADS):
            m8_next = logits(h + 1, kind) if h + 1 < A_HEADS else None
            accumulate(h, m8)
            m8 = m8_next

    @pl.when(j < i - 1)
    def _():
        all_heads("far")

    @pl.when(j == i - 1)
    def _():
        all_heads("sub")

    @pl.when(j == i)
    def _():
        all_heads("diag")

    @pl.when(j == i)
    def _():
        parts = []
        for h in range(A_HEADS):
            hs = slice(h * A_HEAD_DIM, (h + 1) * A_HEAD_DIM)
            out_h = jnp.transpose(acc_sc[hs, :] / l_sc[h:h + 1, :])
            parts.append((out_h * _silu(z_ref[:, hs])).astype(_BF16))
        gated = jnp.concatenate(parts, axis=1)
        o_ref[...] = x_ref[...] + jnp.dot(gated, wo_ref[...], preferred_element_type=_F32)


def _attention(qkv, v_t, words_t, t0, t1, z, x, wo, tq):
    bsz, seq, d = x.shape
    tk = KEY_TILE
    assert tq == tk and seq % tq == 0
    nq = seq // tq
    bt, it, jt = [], [], []
    for b in range(bsz):
        for i in range(nq):
            for j in range(i + 1):
                bt.append(b)
                it.append(i)
                jt.append(j)
    tables = [jnp.asarray(np.asarray(t, np.int32)) for t in (bt, it, jt)]

    grid_spec = pltpu.PrefetchScalarGridSpec(
        num_scalar_prefetch=3,
        grid=(len(bt),),
        in_specs=[
            pl.BlockSpec((None, tq, A_WIDTH), lambda s, b, i, j: (b[s], i[s], 0)),
            pl.BlockSpec((None, tk, A_WIDTH), lambda s, b, i, j: (b[s], j[s], 1)),
            pl.BlockSpec((None, A_HEADS, A_HEAD_DIM + STAGE_PAD, tk),
                         lambda s, b, i, j: (b[s], 0, 0, j[s])),
            pl.BlockSpec((None, tq // LANES, None, 2, 8, LANES),
                         lambda s, b, i, j: (b[s], i[s], j[s], 0, 0, 0)),
            _const_spec(t0.shape),
            _const_spec(t1.shape),
            pl.BlockSpec((None, tq, A_WIDTH), lambda s, b, i, j: (b[s], i[s], 0)),
            pl.BlockSpec((None, tq, d), lambda s, b, i, j: (b[s], i[s], 0)),
            _const_spec(wo.shape),
        ],
        out_specs=pl.BlockSpec((None, tq, d), lambda s, b, i, j: (b[s], i[s], 0)),
        scratch_shapes=[
            pltpu.VMEM((A_HEADS, tq), _F32),
            pltpu.VMEM((A_HEADS, tq), _F32),
            pltpu.VMEM((A_WIDTH, tq), _F32),
            pltpu.VMEM((tk, tq), _F32),
            pltpu.VMEM((tk + STAGE_PAD, tq), _F32),
            pltpu.VMEM((tk + STAGE_PAD, tq), _F32),
            pltpu.VMEM((tk + STAGE_PAD, tq), _BF16),
            pltpu.VMEM((tk + STAGE_PAD, tq), _BF16),
        ],
    )
    return pl.pallas_call(
        functools.partial(_attn_kernel, tq=tq, tk=tk),
        grid_spec=grid_spec,
        out_shape=jax.ShapeDtypeStruct((bsz, seq, d), _F32),
        compiler_params=pltpu.CompilerParams(
            dimension_semantics=("arbitrary",), vmem_limit_bytes=VMEM_LIMIT),
        name="dsa_attention",
    )(*tables, qkv, qkv, v_t, words_t, t0, t1, z, x, wo)


def _sgu_kernel(x_ref, g_ref, win_ref, lng_ref, lnb_ref, ws_ref, bst_ref, wout_ref, fg_ref,
                o_ref, *, tm, width):
    x = x_ref[...]
    hb = _rms_normalize(x, g_ref[...]).astype(_BF16)
    gdim = width // B_GROUPS

    def mm(lo):
        return jnp.dot(hb, win_ref[:, lo:lo + width], preferred_element_type=_F32)

    v = mm(width)
    mu = jnp.mean(v, axis=-1, keepdims=True)
    vc = v - mu
    var = jnp.mean(vc * vc, axis=-1, keepdims=True)
    vn = ((vc * lax.rsqrt(var + LN_EPS)) * lng_ref[...] + lnb_ref[...]).astype(_BF16)

    row = lax.broadcasted_iota(_I32, (SGU_CHUNK, SGU_CHUNK), 0)
    col = lax.broadcasted_iota(_I32, (SGU_CHUNK, SGU_CHUNK), 1)
    bst = bst_ref[...]
    cols = []
    for g in range(B_GROUPS):
        ws = jnp.where(row >= col, ws_ref[g], 0.0).astype(_BF16)
        bias = bst[:, g:g + 1]
        chunks = []
        for c in range(tm // SGU_CHUNK):
            blk = vn[c * SGU_CHUNK:(c + 1) * SGU_CHUNK, g * gdim:(g + 1) * gdim]
            chunks.append(jnp.dot(ws, blk, preferred_element_type=_F32) + bias)
        cols.append(jnp.concatenate(chunks, axis=0))
    mixed = jnp.concatenate(cols, axis=1)

    y = (mm(0) * mixed * _silu(mm(2 * width))).astype(_BF16)
    x2 = x + jnp.dot(y, wout_ref[...], preferred_element_type=_F32)
    o_ref[...] = _rms_normalize(x2, fg_ref[...])


def _sgu_layer(x2, g, w_in, ln_g, ln_b, w_s, b_s_t, w_out, final_g, tm):
    m, d = x2.shape
    width = w_out.shape[0]
    return pl.pallas_call(
        functools.partial(_sgu_kernel, tm=tm, width=width),
        grid=(m // tm,),
        in_specs=[
            pl.BlockSpec((tm, d), lambda i: (i, 0)),
            _const_spec((1, d)),
            _const_spec(w_in.shape),
            _const_spec((1, width)),
            _const_spec((1, width)),
            _const_spec(w_s.shape),
            _const_spec(b_s_t.shape),
            _const_spec(w_out.shape),
            _const_spec((1, d)),
        ],
        out_specs=pl.BlockSpec((tm, d), lambda i: (i, 0)),
        out_shape=jax.ShapeDtypeStruct((m, d), _F32),
        compiler_params=pltpu.CompilerParams(
            dimension_semantics=("arbitrary",), vmem_limit_bytes=VMEM_LIMIT),
        name="sgu_layer_final_norm",
    )(x2, g, w_in, ln_g, ln_b, w_s, b_s_t, w_out, final_g)


def _t5_bucket(rel):
    half = REL_BUCKETS // 2
    max_exact = half // 2
    ret = jnp.where(rel < 0, half, 0)
    n = jnp.abs(rel)
    nf = jnp.maximum(n, 1).astype(jnp.float32)
    large = max_exact + (jnp.log(nf / max_exact) / math.log(REL_MAX_DIST / max_exact)
                         * (half - max_exact)).astype(jnp.int32)
    large = jnp.minimum(large, half - 1)
    return ret + jnp.where(n < max_exact, n, large)


def _bias_tables(rel_bias):
    a = jnp.arange(LANES, dtype=jnp.int32)
    rel0 = a[None, :] - a[:, None]
    far = rel_bias[_t5_bucket(jnp.int32(4 * REL_MAX_DIST))]
    t0 = jnp.transpose(rel_bias[_t5_bucket(rel0)] - far, (2, 0, 1)) * LOG2_E
    t1 = jnp.transpose(rel_bias[_t5_bucket(rel0 + LANES)] - far, (2, 0, 1)) * LOG2_E
    return t0, t1


def kernel(x, norm_g, final_g, rel_bias, a_w_in, a_w_out, b_w_in, b_ln_g, b_ln_b, b_w_s,
           b_b_s, b_w_out):
    bsz, seq, d = x.shape
    assert norm_g.shape[0] == 2 and a_w_in.shape[0] == 1 and b_w_in.shape[0] == 1
    m = bsz * seq
    n_qi = IDX_HEADS * IDX_DIM
    topk = min(TOPK_MAX, seq // 4)

    cols = a_w_in.shape[2]
    w_pad = jnp.pad(a_w_in[0], ((0, 0), (0, 4 * A_WIDTH + n_qi + LANES - cols))).astype(_BF16)
    qkv, z, qi, tail = _proj_a(x.reshape(m, d), norm_g[0:1], w_pad, tm=512)
    qi_t = jnp.transpose(qi.reshape(bsz, seq, IDX_HEADS, IDX_DIM), (0, 2, 3, 1))
    ki = tail[:, :IDX_DIM].reshape(bsz, seq, IDX_DIM).astype(_BF16)
    wi_t = jnp.transpose(tail[:, IDX_DIM:IDX_DIM + IDX_HEADS].reshape(bsz, seq, IDX_HEADS),
                         (0, 2, 1))
    words = _indexer(ki, qi_t, wi_t, tq=KEY_TILE, topk=topk)
    t0, t1 = _bias_tables(rel_bias)
    qkv = qkv.reshape(bsz, seq, 3 * A_WIDTH)
    v_t = jnp.transpose(qkv[:, :, 2 * A_WIDTH:], (0, 2, 1)).reshape(bsz, A_HEADS, A_HEAD_DIM, seq)
    ones_rows = jnp.zeros((bsz, A_HEADS, STAGE_PAD, seq), _BF16).at[:, :, 0, :].set(1.0)
    v_t = jnp.concatenate([v_t, ones_rows], axis=2)
    x1 = _attention(qkv, v_t, words, t0, t1,
                    z.reshape(bsz, seq, A_WIDTH), x, a_w_out[0].astype(_BF16), tq=KEY_TILE)

    out = _sgu_layer(x1.reshape(m, d), norm_g[1:2], b_w_in[0].astype(_BF16),
                     b_ln_g[0:1], b_ln_b[0:1], b_w_s[0], jnp.transpose(b_b_s[0]),
                     b_w_out[0].astype(_BF16), final_g.reshape(1, d), tm=256)
    return out.reshape(bsz, seq, d)
```

```python
import functools
import math

import numpy as np
import jax
import jax.numpy as jnp
from jax import lax
from jax.experimental import pallas as pl
from jax.experimental.pallas import tpu as pltpu

CHUNK = 64
RMS_EPS = 1e-6
LN_EPS = 1e-5

A_HEADS = 8
A_HEAD_DIM = 128
A_WIDTH = A_HEADS * A_HEAD_DIM
IDX_HEADS = 8
IDX_DIM = 64
TOPK_MAX = 256

REL_BUCKETS = 32
REL_MAX_DIST = 128

SGU_CHUNK = 128
B_GROUPS = 8

LANES = 128
KEY_TILE = 512
WORD_BITS = 32
RADIX_GROUP = 4
STAGE_PAD = 16
VMEM_LIMIT = 56 * 1024 * 1024

LOG2_E = math.log2(math.e)
Q_SCALE = A_HEAD_DIM ** -0.5 * LOG2_E

INT_MIN = -2 ** 31
INT_MAX = 2 ** 31 - 1
NEG_BIG = -1e30

_BF16 = jnp.bfloat16
_F32 = jnp.float32
_I32 = jnp.int32


def _const_spec(shape):
    zeros = (0,) * len(shape)
    return pl.BlockSpec(shape, lambda *_: zeros, pipeline_mode=pl.Buffered(1))


def _rms_normalize(x, g):
    return x * lax.rsqrt(jnp.mean(x * x, axis=-1, keepdims=True) + RMS_EPS) * g


def _silu(z):
    return z / (1.0 + jnp.exp(-z))


def _proj_a_kernel(x_ref, g_ref, w_ref, qkv_ref, z_ref, qi_ref, tail_ref):
    hb = _rms_normalize(x_ref[...], g_ref[...]).astype(_BF16)

    def mm(lo, width):
        return jnp.dot(hb, w_ref[:, lo:lo + width], preferred_element_type=_F32)

    step = 512
    for c in range(0, 3 * A_WIDTH, step):
        scale = Q_SCALE if c < A_WIDTH else 1.0
        qkv_ref[:, c:c + step] = (mm(c, step) * scale).astype(_BF16)
    for c in range(0, A_WIDTH, step):
        z_ref[:, c:c + step] = mm(3 * A_WIDTH + c, step)
    qi_ref[...] = mm(4 * A_WIDTH, IDX_HEADS * IDX_DIM).astype(_BF16)
    tail_ref[...] = mm(4 * A_WIDTH + IDX_HEADS * IDX_DIM, LANES)


def _proj_a(x2, g, w_pad, tm):
    m, d = x2.shape
    n_qi = IDX_HEADS * IDX_DIM
    return pl.pallas_call(
        _proj_a_kernel,
        grid=(m // tm,),
        in_specs=[
            pl.BlockSpec((tm, d), lambda i: (i, 0)),
            _const_spec((1, d)),
            _const_spec(w_pad.shape),
        ],
        out_specs=[
            pl.BlockSpec((tm, 3 * A_WIDTH), lambda i: (i, 0)),
            pl.BlockSpec((tm, A_WIDTH), lambda i: (i, 0)),
            pl.BlockSpec((tm, n_qi), lambda i: (i, 0)),
            pl.BlockSpec((tm, LANES), lambda i: (i, 0)),
        ],
        out_shape=[
            jax.ShapeDtypeStruct((m, 3 * A_WIDTH), _BF16),
            jax.ShapeDtypeStruct((m, A_WIDTH), _F32),
            jax.ShapeDtypeStruct((m, n_qi), _BF16),
            jax.ShapeDtypeStruct((m, LANES), _F32),
        ],
        compiler_params=pltpu.CompilerParams(
            dimension_semantics=("arbitrary",), vmem_limit_bytes=VMEM_LIMIT),
        name="dsa_in_proj",
    )(x2, g, w_pad)


def _bit_transpose(vregs):
    a = list(reversed(vregs))
    j, mask = 16, 0x0000FFFF
    while j:
        k = 0
        while k < WORD_BITS:
            t = (a[k] ^ lax.shift_right_logical(a[k + j], jnp.int32(j))) & mask
            a[k] = a[k] ^ t
            a[k + j] = a[k + j] ^ (t << j)
            k = (k + j + 1) & ~j
        j >>= 1
        mask = (mask ^ (mask << j)) & 0xFFFFFFFF
        if mask >= 2 ** 31:
            mask -= 2 ** 32
    return list(reversed(a))


def _indexer_kernel(ki_ref, qit_ref, wit_ref, out_ref, p_sc, c_sc, g_sc, u0_sc, u1_sc,
                    *, tq, topk):
    i = pl.program_id(1)
    nlg = tq // LANES
    half = KEY_TILE // 2
    nj = i + 1
    wscaled = wit_ref[...] * (IDX_HEADS ** -0.5 * IDX_DIM ** -0.5)

    rows = 64
    u_bufs = (u0_sc, u1_sc)
    off = pl.multiple_of(jnp.minimum(i, 0) * 8, 8)

    def planes_of_tile(j, diagonal):
        base = pl.multiple_of(j * KEY_TILE, KEY_TILE)
        for hf in range(2):
            u_buf = u_bufs[hf]
            for c in range(half // rows):
                r0 = hf * half + c * rows
                kt = ki_ref[pl.ds(base + r0, rows), :]
                acc = jnp.zeros((rows, tq), _F32)
                for h in range(IDX_HEADS):
                    d = jnp.dot(kt, qit_ref[h], preferred_element_type=_F32)
                    acc = acc + jnp.maximum(d, 0.0) * wscaled[h:h + 1, :]
                bits = lax.bitcast_convert_type(acc, _I32)
                u = bits ^ ((bits >> 31) | INT_MIN)
                if diagonal:
                    s_chunk = (r0 + lax.broadcasted_iota(_I32, (rows, tq), 0)) // CHUNK
                    t_chunk = lax.broadcasted_iota(_I32, (rows, tq), 1) // CHUNK
                    u = jnp.where(s_chunk <= t_chunk, u, 0)
                u_buf[pl.ds(off + c * rows, rows), :] = u
            for lg in range(nlg):
                blk = [u_buf[pl.ds(off + 8 * m, 8), lg * LANES:(lg + 1) * LANES]
                       for m in range(WORD_BITS)]
                planes = _bit_transpose(blk)
                for b in range(WORD_BITS):
                    p_sc[j, hf, lg, 8 * b:8 * b + 8, :] = planes[b]

    def fill(j, carry):
        planes_of_tile(j, diagonal=False)
        return carry

    @pl.when((pl.program_id(0) == 0) & (i == 0))
    def _():
        def zero_tile(j, carry):
            p_sc[j] = jnp.zeros(p_sc.shape[1:], _I32)
            return carry
        lax.fori_loop(0, p_sc.shape[0], zero_tile, 0)

    lax.fori_loop(0, i, fill, 0)
    planes_of_tile(i, diagonal=True)

    ntiles = p_sc.shape[0]
    ngroups = lax.div(nj + RADIX_GROUP - 1, RADIX_GROUP)

    def init(j, carry):
        c_sc[j] = jnp.full(c_sc.shape[1:], jnp.where(j < nj, -1, 0), _I32)
        g_sc[j] = jnp.zeros(g_sc.shape[1:], _I32)
        return carry

    lax.fori_loop(0, jnp.minimum(ngroups * RADIX_GROUP, ntiles), init, 0)

    def per_query_sum(parts):
        return [jnp.broadcast_to(jnp.sum(p, axis=0, keepdims=True), (8, LANES)) for p in parts]

    def count(word_fn):
        def body(j, acc):
            acc = list(acc)
            for hf in range(2):
                for lg in range(nlg):
                    acc[lg] = acc[lg] + lax.population_count(word_fn(j, hf, lg))
            return tuple(acc)
        zero = jnp.zeros((8, LANES), _I32)
        return per_query_sum(lax.fori_loop(0, nj, body, (zero,) * nlg))

    def radix_step(step, carry):
        need, found, ones = carry
        b = WORD_BITS - 1 - step
        b8 = pl.multiple_of(b * 8, 8)
        next8 = pl.multiple_of(jnp.maximum(b - 1, 0) * 8, 8)
        take = [ones[lg] >= need[lg] for lg in range(nlg)]

        def sweep(grp, acc):
            acc = list(acc)
            for jj in range(RADIX_GROUP):
                j = grp * RADIX_GROUP + jj
                for hf in range(2):
                    for lg in range(nlg):
                        cand = c_sc[j, hf, lg]
                        hit = cand & p_sc[j, hf, lg, pl.ds(b8, 8), :]
                        keep = jnp.where(take[lg], hit, cand ^ hit)
                        c_sc[j, hf, lg] = keep
                        g_sc[j, hf, lg] = g_sc[j, hf, lg] | jnp.where(take[lg], 0, hit)
                        acc[lg] = acc[lg] + lax.population_count(
                            keep & p_sc[j, hf, lg, pl.ds(next8, 8), :])
            return tuple(acc)

        zero = jnp.zeros((8, LANES), _I32)
        nxt = per_query_sum(lax.fori_loop(0, ngroups, sweep, (zero,) * nlg))
        need = tuple(jnp.where(take[lg], need[lg], need[lg] - ones[lg]) for lg in range(nlg))
        found = tuple(jnp.where(take[lg], 1, found[lg]) for lg in range(nlg))
        return need, found, tuple(nxt)

    need0 = (jnp.full((8, LANES), topk, _I32),) * nlg
    found0 = (jnp.zeros((8, LANES), _I32),) * nlg
    top8 = (WORD_BITS - 1) * 8
    ones0 = tuple(count(lambda j, hf, lg: p_sc[j, hf, lg, top8:top8 + 8, :]))
    need, found, _ = lax.fori_loop(0, WORD_BITS, radix_step, (need0, found0, ones0))
    found = [f != 0 for f in found]

    n_ties = count(lambda j, hf, lg: c_sc[j, hf, lg])
    straddle = [found[lg] & (n_ties[lg] > need[lg]) for lg in range(nlg)]
    any_straddle = functools.reduce(
        jnp.maximum, [jnp.max(jnp.where(s, 1.0, 0.0)) for s in straddle]) > 0.0
    sub = lax.broadcasted_iota(_I32, (8, LANES), 0)

    def upto(limit, j, hf):
        n = jnp.clip(((limit - (j * KEY_TILE + hf * half + sub)) >> 3) + 1, 0, WORD_BITS)
        return jnp.where(n >= WORD_BITS, -1, (jnp.int32(1) << jnp.minimum(n, WORD_BITS - 1)) - 1)

    def tie_limit():
        nbits = max(1, int(math.ceil(math.log2(p_sc.shape[0] * KEY_TILE))))

        def step(t, limit):
            cand = [limit[lg] | (jnp.int32(1) << (jnp.int32(nbits - 1) - t)) for lg in range(nlg)]
            before = count(lambda j, hf, lg: c_sc[j, hf, lg] & upto(cand[lg] - 1, j, hf))
            return tuple(jnp.where(before[lg] < need[lg], cand[lg], limit[lg])
                         for lg in range(nlg))

        limit = lax.fori_loop(0, nbits, step, (jnp.zeros((8, LANES), _I32),) * nlg)
        return tuple(jnp.where(straddle[lg], limit[lg], INT_MAX) for lg in range(nlg))

    limit = lax.cond(any_straddle, tie_limit,
                     lambda: (jnp.full((8, LANES), INT_MAX, _I32),) * nlg)

    def emit(j, carry):
        for hf in range(2):
            for lg in range(nlg):
                ties = jnp.where(found[lg], c_sc[j, hf, lg] & upto(limit[lg], j, hf), 0)
                out_ref[lg, j, hf] = g_sc[j, hf, lg] | ties
        return carry

    lax.fori_loop(0, nj, emit, 0)

    def clear(j, carry):
        for lg in range(nlg):
            out_ref[lg, j] = jnp.zeros((2, 8, LANES), _I32)
        return carry

    lax.fori_loop(nj, p_sc.shape[0], clear, 0)


def _indexer(ki, qi_t, wi_t, tq, topk):
    bsz, seq, _ = ki.shape
    nt = seq // KEY_TILE
    nlg = tq // LANES
    assert seq % KEY_TILE == 0 and tq == KEY_TILE and KEY_TILE % (2 * 8 * WORD_BITS) == 0
    assert nt % RADIX_GROUP == 0
    return pl.pallas_call(
        functools.partial(_indexer_kernel, tq=tq, topk=topk),
        grid=(bsz, seq // tq),
        in_specs=[
            pl.BlockSpec((None, seq, IDX_DIM), lambda b, i: (b, 0, 0),
                         pipeline_mode=pl.Buffered(1)),
            pl.BlockSpec((None, IDX_HEADS, IDX_DIM, tq), lambda b, i: (b, 0, 0, i)),
            pl.BlockSpec((None, IDX_HEADS, tq), lambda b, i: (b, 0, i)),
        ],
        out_specs=pl.BlockSpec((None, nlg, nt, 2, 8, LANES), lambda b, i: (b, i, 0, 0, 0, 0)),
        out_shape=jax.ShapeDtypeStruct((bsz, seq // LANES, nt, 2, 8, LANES), _I32),
        scratch_shapes=[
            pltpu.VMEM((nt, 2, nlg, KEY_TILE // 2, LANES), _I32),
            pltpu.VMEM((nt, 2, nlg, 8, LANES), _I32),
            pltpu.VMEM((nt, 2, nlg, 8, LANES), _I32),
            pltpu.VMEM((KEY_TILE // 2 + 8, tq), _I32),
            pltpu.VMEM((KEY_TILE // 2 + 8, tq), _I32),
        ],
        compiler_params=pltpu.CompilerParams(
            dimension_semantics=("arbitrary", "arbitrary"), vmem_limit_bytes=VMEM_LIMIT),
        name="dsa_indexer_topk",
    )(ki, qi_t, wi_t)


def _attn_kernel(bt_ref, it_ref, jt_ref, q_ref, k_ref, vt_ref, words_ref, t0_ref, t1_ref,
                 z_ref, x_ref, wo_ref, o_ref, m_sc, l_sc, acc_sc, mask_sc, s0_sc, s1_sc, p0_sc, p1_sc,
                 *, tq, tk):
    step = pl.program_id(0)
    i = it_ref[step]
    j = jt_ref[step]
    nsub = tq // LANES

    @pl.when(j == 0)
    def _():
        m_sc[...] = jnp.full(m_sc.shape, NEG_BIG, _F32)
        l_sc[...] = jnp.zeros(l_sc.shape, _F32)
        acc_sc[...] = jnp.zeros(acc_sc.shape, _F32)

    def build_mask():
        for hf in range(2):
            for lg in range(tq // LANES):
                w = words_ref[lg, hf]
                for m in range(WORD_BITS):
                    r0 = hf * (tk // 2) + 8 * m
                    mask_sc[r0:r0 + 8, lg * LANES:(lg + 1) * LANES] = jnp.where(
                        (w << (WORD_BITS - 1 - m)) < 0, 0.0, -jnp.inf)

    rows = 64
    s_bufs = (s0_sc, s1_sc)
    p_bufs = (p0_sc, p1_sc)
    off = pl.multiple_of(jnp.minimum(j, 0) * STAGE_PAD, STAGE_PAD)

    def fold_rows(x, op):
        return op(x.reshape(x.shape[0] // 8, 8, tq), axis=0)

    def bias_block(h, kind, key_blk, qry_blk, r):
        if kind == "diag" and key_blk == qry_blk:
            return t0_ref[h, r:r + rows, :]
        if kind == "diag" and key_blk + 1 == qry_blk:
            return t1_ref[h, r:r + rows, :]
        if kind == "sub" and key_blk == nsub - 1 and qry_blk == 0:
            return t1_ref[h, r:r + rows, :]
        return None

    def logits(h, kind):
        hs = slice(h * A_HEAD_DIM, (h + 1) * A_HEAD_DIM)
        s_buf = s_bufs[h % 2]
        s = lax.dot_general(k_ref[:, hs], q_ref[:, hs], (((1,), (1,)), ((), ())),
                            preferred_element_type=_F32)
        m8 = None
        for r0 in range(0, tk, rows):
            t = s[r0:r0 + rows, :] + mask_sc[r0:r0 + rows, :]
            if kind != "far":
                parts = []
                for c in range(nsub):
                    blk = t[:, c * LANES:(c + 1) * LANES]
                    bias = bias_block(h, kind, r0 // LANES, c, r0 % LANES)
                    parts.append(blk if bias is None else blk + bias)
                t = jnp.concatenate(parts, axis=1)
            s_buf[pl.ds(off + r0, rows), :] = t
            m8 = fold_rows(t, jnp.max) if m8 is None else jnp.maximum(m8, fold_rows(t, jnp.max))
        return m8

    def accumulate(h, m8):
        hs = slice(h * A_HEAD_DIM, (h + 1) * A_HEAD_DIM)
        s_buf, p_buf = s_bufs[h % 2], p_bufs[h % 2]
        m_prev = m_sc[h:h + 1, :]
        m_new = jnp.maximum(m_prev, jnp.max(m8, axis=0, keepdims=True))
        alpha = jnp.exp2(m_prev - m_new)
        for r0 in range(0, tk, rows):
            p = jnp.exp2(s_buf[pl.ds(off + r0, rows), :] - m_new)
            p_buf[pl.ds(off + r0, rows), :] = p.astype(_BF16)
        pv = jnp.dot(vt_ref[h], p_buf[pl.ds(off, tk), :], preferred_element_type=_F32)
        acc_sc[hs, :] = alpha * acc_sc[hs, :] + pv[0:A_HEAD_DIM, :]
        l_sc[h:h + 1, :] = alpha * l_sc[h:h + 1, :] + pv[A_HEAD_DIM:A_HEAD_DIM + 1, :]
        m_sc[h:h + 1, :] = m_new

    def all_heads(kind):
        build_mask()
        m8 = logits(0, kind)
        for h in range(A_HEADS):
            m8_next = logits(h + 1, kind) if h + 1 < A_HEADS else None
            accumulate(h, m8)
            m8 = m8_next

    @pl.when(j < i - 1)
    def _():
        all_heads("far")

    @pl.when(j == i - 1)
    def _():
        all_heads("sub")

    @pl.when(j == i)
    def _():
        all_heads("diag")

    @pl.when(j == i)
    def _():
        parts = []
        for h in range(A_HEADS):
            hs = slice(h * A_HEAD_DIM, (h + 1) * A_HEAD_DIM)
            out_h = jnp.transpose(acc_sc[hs, :] / l_sc[h:h + 1, :])
            parts.append((out_h * _silu(z_ref[:, hs])).astype(_BF16))
        gated = jnp.concatenate(parts, axis=1)
        o_ref[...] = x_ref[...] + jnp.dot(gated, wo_ref[...], preferred_element_type=_F32)


def _attention(qkv, v_t, words_t, t0, t1, z, x, wo, tq):
    bsz, seq, d = x.shape
    tk = KEY_TILE
    assert tq == tk and seq % tq == 0
    nq = seq // tq
    bt, it, jt = [], [], []
    for b in range(bsz):
        for i in range(nq):
            for j in range(i + 1):
                bt.append(b)
                it.append(i)
                jt.append(j)
    tables = [jnp.asarray(np.asarray(t, np.int32)) for t in (bt, it, jt)]

    grid_spec = pltpu.PrefetchScalarGridSpec(
        num_scalar_prefetch=3,
        grid=(len(bt),),
        in_specs=[
            pl.BlockSpec((None, tq, A_WIDTH), lambda s, b, i, j: (b[s], i[s], 0)),
            pl.BlockSpec((None, tk, A_WIDTH), lambda s, b, i, j: (b[s], j[s], 1)),
            pl.BlockSpec((None, A_HEADS, A_HEAD_DIM + STAGE_PAD, tk),
                         lambda s, b, i, j: (b[s], 0, 0, j[s])),
            pl.BlockSpec((None, tq // LANES, None, 2, 8, LANES),
                         lambda s, b, i, j: (b[s], i[s], j[s], 0, 0, 0)),
            _const_spec(t0.shape),
            _const_spec(t1.shape),
            pl.BlockSpec((None, tq, A_WIDTH), lambda s, b, i, j: (b[s], i[s], 0)),
            pl.BlockSpec((None, tq, d), lambda s, b, i, j: (b[s], i[s], 0)),
            _const_spec(wo.shape),
        ],
        out_specs=pl.BlockSpec((None, tq, d), lambda s, b, i, j: (b[s], i[s], 0)),
        scratch_shapes=[
            pltpu.VMEM((A_HEADS, tq), _F32),
            pltpu.VMEM((A_HEADS, tq), _F32),
            pltpu.VMEM((A_WIDTH, tq), _F32),
            pltpu.VMEM((tk, tq), _F32),
            pltpu.VMEM((tk + STAGE_PAD, tq), _F32),
            pltpu.VMEM((tk + STAGE_PAD, tq), _F32),
            pltpu.VMEM((tk + STAGE_PAD, tq), _BF16),
            pltpu.VMEM((tk + STAGE_PAD, tq), _BF16),
        ],
    )
    return pl.pallas_call(
        functools.partial(_attn_kernel, tq=tq, tk=tk),
        grid_spec=grid_spec,
        out_shape=jax.ShapeDtypeStruct((bsz, seq, d), _F32),
        compiler_params=pltpu.CompilerParams(
            dimension_semantics=("arbitrary",), vmem_limit_bytes=VMEM_LIMIT),
        name="dsa_attention",
    )(*tables, qkv, qkv, v_t, words_t, t0, t1, z, x, wo)


def _sgu_kernel(x_ref, g_ref, win_ref, lng_ref, lnb_ref, ws_ref, bst_ref, wout_ref, fg_ref,
                o_ref, *, tm, width):
    x = x_ref[...]
    hb = _rms_normalize(x, g_ref[...]).astype(_BF16)
    gdim = width // B_GROUPS

    def mm(lo):
        return jnp.dot(hb, win_ref[:, lo:lo + width], preferred_element_type=_F32)

    v = mm(width)
    mu = jnp.mean(v, axis=-1, keepdims=True)
    vc = v - mu
    var = jnp.mean(vc * vc, axis=-1, keepdims=True)
    vn = ((vc * lax.rsqrt(var + LN_EPS)) * lng_ref[...] + lnb_ref[...]).astype(_BF16)

    row = lax.broadcasted_iota(_I32, (SGU_CHUNK, SGU_CHUNK), 0)
    col = lax.broadcasted_iota(_I32, (SGU_CHUNK, SGU_CHUNK), 1)
    bst = bst_ref[...]
    cols = []
    for g in range(B_GROUPS):
        ws = jnp.where(row >= col, ws_ref[g], 0.0).astype(_BF16)
        bias = bst[:, g:g + 1]
        chunks = []
        for c in range(tm // SGU_CHUNK):
            blk = vn[c * SGU_CHUNK:(c + 1) * SGU_CHUNK, g * gdim:(g + 1) * gdim]
            chunks.append(jnp.dot(ws, blk, preferred_element_type=_F32) + bias)
        cols.append(jnp.concatenate(chunks, axis=0))
    mixed = jnp.concatenate(cols, axis=1)

    y = (mm(0) * mixed * _silu(mm(2 * width))).astype(_BF16)
    x2 = x + jnp.dot(y, wout_ref[...], preferred_element_type=_F32)
    o_ref[...] = _rms_normalize(x2, fg_ref[...])


def _sgu_layer(x2, g, w_in, ln_g, ln_b, w_s, b_s_t, w_out, final_g, tm):
    m, d = x2.shape
    width = w_out.shape[0]
    return pl.pallas_call(
        functools.partial(_sgu_kernel, tm=tm, width=width),
        grid=(m // tm,),
        in_specs=[
            pl.BlockSpec((tm, d), lambda i: (i, 0)),
            _const_spec((1, d)),
            _const_spec(w_in.shape),
            _const_spec((1, width)),
            _const_spec((1, width)),
            _const_spec(w_s.shape),
            _const_spec(b_s_t.shape),
            _const_spec(w_out.shape),
            _const_spec((1, d)),
        ],
        out_specs=pl.BlockSpec((tm, d), lambda i: (i, 0)),
        out_shape=jax.ShapeDtypeStruct((m, d), _F32),
        compiler_params=pltpu.CompilerParams(
            dimension_semantics=("arbitrary",), vmem_limit_bytes=VMEM_LIMIT),
        name="sgu_layer_final_norm",
    )(x2, g, w_in, ln_g, ln_b, w_s, b_s_t, w_out, final_g)


def _t5_bucket(rel):
    half = REL_BUCKETS // 2
    max_exact = half // 2
    ret = jnp.where(rel < 0, half, 0)
    n = jnp.abs(rel)
    nf = jnp.maximum(n, 1).astype(jnp.float32)
    large = max_exact + (jnp.log(nf / max_exact) / math.log(REL_MAX_DIST / max_exact)
                         * (half - max_exact)).astype(jnp.int32)
    large = jnp.minimum(large, half - 1)
    return ret + jnp.where(n < max_exact, n, large)


def _bias_tables(rel_bias):
    n = LANES
    far = rel_bias[_t5_bucket(jnp.int32(4 * REL_MAX_DIST))]

    def toeplitz(first_rel):
        rel = first_rel + jnp.arange(2 * n - 1, dtype=jnp.int32)
        vals = jnp.transpose((rel_bias[_t5_bucket(rel)] - far) * LOG2_E)
        hankel = jnp.tile(vals, (1, n + 1))[:, :n * 2 * n].reshape(-1, n, 2 * n)[:, :, :n]
        return hankel[:, ::-1, :]

    return toeplitz(-(n - 1)), toeplitz(1)


def kernel(x, norm_g, final_g, rel_bias, a_w_in, a_w_out, b_w_in, b_ln_g, b_ln_b, b_w_s,
           b_b_s, b_w_out):
    bsz, seq, d = x.shape
    assert norm_g.shape[0] == 2 and a_w_in.shape[0] == 1 and b_w_in.shape[0] == 1
    m = bsz * seq
    n_qi = IDX_HEADS * IDX_DIM
    topk = min(TOPK_MAX, seq // 4)

    cols = a_w_in.shape[2]
    w_pad = jnp.pad(a_w_in[0], ((0, 0), (0, 4 * A_WIDTH + n_qi + LANES - cols))).astype(_BF16)
    qkv, z, qi, tail = _proj_a(x.reshape(m, d), norm_g[0:1], w_pad, tm=512)
    qi_t = jnp.transpose(qi.reshape(bsz, seq, IDX_HEADS, IDX_DIM), (0, 2, 3, 1))
    ki = tail[:, :IDX_DIM].reshape(bsz, seq, IDX_DIM).astype(_BF16)
    wi_t = jnp.transpose(tail[:, IDX_DIM:IDX_DIM + IDX_HEADS].reshape(bsz, seq, IDX_HEADS),
                         (0, 2, 1))
    words = _indexer(ki, qi_t, wi_t, tq=KEY_TILE, topk=topk)
    t0, t1 = _bias_tables(rel_bias)
    qkv = qkv.reshape(bsz, seq, 3 * A_WIDTH)
    v_t = jnp.transpose(qkv[:, :, 2 * A_WIDTH:], (0, 2, 1)).reshape(bsz, A_HEADS, A_HEAD_DIM, seq)
    ones_rows = jnp.zeros((bsz, A_HEADS, STAGE_PAD, seq), _BF16).at[:, :, 0, :].set(1.0)
    v_t = jnp.concatenate([v_t, ones_rows], axis=2)
    x1 = _attention(qkv, v_t, words, t0, t1,
                    z.reshape(bsz, seq, A_WIDTH), x, a_w_out[0].astype(_BF16), tq=KEY_TILE)

    out = _sgu_layer(x1.reshape(m, d), norm_g[1:2], b_w_in[0].astype(_BF16),
                     b_ln_g[0:1], b_ln_b[0:1], b_w_s[0], jnp.transpose(b_b_s[0]),
                     b_w_out[0].astype(_BF16), final_g.reshape(1, d), tm=256)
    return out.reshape(bsz, seq, d)
```

```python
import functools
import math

import numpy as np
import jax
import jax.numpy as jnp
from jax import lax
from jax.experimental import pallas as pl
from jax.experimental.pallas import tpu as pltpu

CHUNK = 64
RMS_EPS = 1e-6
LN_EPS = 1e-5

A_HEADS = 8
A_HEAD_DIM = 128
A_WIDTH = A_HEADS * A_HEAD_DIM
IDX_HEADS = 8
IDX_DIM = 64
TOPK_MAX = 256

REL_BUCKETS = 32
REL_MAX_DIST = 128

SGU_CHUNK = 128
B_GROUPS = 8

LANES = 128
KEY_TILE = 512
WORD_BITS = 32
RADIX_GROUP = 4
STAGE_PAD = 16
VMEM_LIMIT = 56 * 1024 * 1024

LOG2_E = math.log2(math.e)
Q_SCALE = A_HEAD_DIM ** -0.5 * LOG2_E

INT_MIN = -2 ** 31
INT_MAX = 2 ** 31 - 1
NEG_BIG = -1e30

_BF16 = jnp.bfloat16
_F32 = jnp.float32
_I32 = jnp.int32


def _const_spec(shape):
    zeros = (0,) * len(shape)
    return pl.BlockSpec(shape, lambda *_: zeros, pipeline_mode=pl.Buffered(1))


def _rms_normalize(x, g):
    return x * lax.rsqrt(jnp.mean(x * x, axis=-1, keepdims=True) + RMS_EPS) * g


def _silu(z):
    return z / (1.0 + jnp.exp(-z))


def _proj_a_kernel(x_ref, g_ref, w_ref, qk_ref, vt_ref, z_ref, qit_ref, ki_ref, wit_ref):
    hb = _rms_normalize(x_ref[...], g_ref[...]).astype(_BF16)
    tm = hb.shape[0]

    def mm(lo, width):
        return jnp.dot(hb, w_ref[:, lo:lo + width], preferred_element_type=_F32)

    step = 512
    for c in range(0, 2 * A_WIDTH, step):
        scale = Q_SCALE if c < A_WIDTH else 1.0
        qk_ref[:, c:c + step] = (mm(c, step) * scale).astype(_BF16)
    pad_rows = lax.broadcasted_iota(_I32, (STAGE_PAD, tm), 0)
    for c in range(0, A_WIDTH, step):
        v_c = mm(2 * A_WIDTH + c, step)
        for hh in range(step // A_HEAD_DIM):
            h = c // A_HEAD_DIM + hh
            v_h = v_c[:, hh * A_HEAD_DIM:(hh + 1) * A_HEAD_DIM]
            vt_ref[h, 0:A_HEAD_DIM, :] = jnp.transpose(v_h).astype(_BF16)
            vt_ref[h, A_HEAD_DIM:, :] = jnp.where(pad_rows == 0, 1.0, 0.0).astype(_BF16)
    for c in range(0, A_WIDTH, step):
        z_ref[:, c:c + step] = mm(3 * A_WIDTH + c, step)
    qi_t = jnp.transpose(mm(4 * A_WIDTH, IDX_HEADS * IDX_DIM))
    for h in range(IDX_HEADS):
        qit_ref[h] = qi_t[h * IDX_DIM:(h + 1) * IDX_DIM, :].astype(_BF16)
    tail = mm(4 * A_WIDTH + IDX_HEADS * IDX_DIM, LANES)
    ki_ref[...] = tail[:, 0:IDX_DIM].astype(_BF16)
    wit_ref[...] = jnp.transpose(tail)[IDX_DIM:IDX_DIM + IDX_HEADS, :]


def _proj_a(x, g, w_pad, tm):
    bsz, seq, d = x.shape
    row = lambda b, i: (b, i, 0)
    return pl.pallas_call(
        _proj_a_kernel,
        grid=(bsz, seq // tm),
        in_specs=[
            pl.BlockSpec((None, tm, d), row),
            _const_spec((1, d)),
            _const_spec(w_pad.shape),
        ],
        out_specs=[
            pl.BlockSpec((None, tm, 2 * A_WIDTH), row),
            pl.BlockSpec((None, A_HEADS, A_HEAD_DIM + STAGE_PAD, tm), lambda b, i: (b, 0, 0, i)),
            pl.BlockSpec((None, tm, A_WIDTH), row),
            pl.BlockSpec((None, IDX_HEADS, IDX_DIM, tm), lambda b, i: (b, 0, 0, i)),
            pl.BlockSpec((None, tm, IDX_DIM), row),
            pl.BlockSpec((None, IDX_HEADS, tm), lambda b, i: (b, 0, i)),
        ],
        out_shape=[
            jax.ShapeDtypeStruct((bsz, seq, 2 * A_WIDTH), _BF16),
            jax.ShapeDtypeStruct((bsz, A_HEADS, A_HEAD_DIM + STAGE_PAD, seq), _BF16),
            jax.ShapeDtypeStruct((bsz, seq, A_WIDTH), _F32),
            jax.ShapeDtypeStruct((bsz, IDX_HEADS, IDX_DIM, seq), _BF16),
            jax.ShapeDtypeStruct((bsz, seq, IDX_DIM), _BF16),
            jax.ShapeDtypeStruct((bsz, IDX_HEADS, seq), _F32),
        ],
        compiler_params=pltpu.CompilerParams(
            dimension_semantics=("arbitrary", "arbitrary"), vmem_limit_bytes=VMEM_LIMIT),
        name="dsa_in_proj",
    )(x, g, w_pad)


def _bit_transpose(vregs):
    a = list(reversed(vregs))
    j, mask = 16, 0x0000FFFF
    while j:
        k = 0
        while k < WORD_BITS:
            t = (a[k] ^ lax.shift_right_logical(a[k + j], jnp.int32(j))) & mask
            a[k] = a[k] ^ t
            a[k + j] = a[k + j] ^ (t << j)
            k = (k + j + 1) & ~j
        j >>= 1
        mask = (mask ^ (mask << j)) & 0xFFFFFFFF
        if mask >= 2 ** 31:
            mask -= 2 ** 32
    return list(reversed(a))


def _indexer_kernel(ki_ref, qit_ref, wit_ref, out_ref, p_sc, c_sc, g_sc, u0_sc, u1_sc,
                    *, tq, topk):
    i = pl.program_id(1)
    nlg = tq // LANES
    half = KEY_TILE // 2
    nj = i + 1
    wscaled = wit_ref[...] * (IDX_HEADS ** -0.5 * IDX_DIM ** -0.5)

    rows = 64
    u_bufs = (u0_sc, u1_sc)
    off = pl.multiple_of(jnp.minimum(i, 0) * 8, 8)

    def planes_of_tile(j, diagonal):
        base = pl.multiple_of(j * KEY_TILE, KEY_TILE)
        for hf in range(2):
            u_buf = u_bufs[hf]
            for c in range(half // rows):
                r0 = hf * half + c * rows
                kt = ki_ref[pl.ds(base + r0, rows), :]
                acc = jnp.zeros((rows, tq), _F32)
                for h in range(IDX_HEADS):
                    d = jnp.dot(kt, qit_ref[h], preferred_element_type=_F32)
                    acc = acc + jnp.maximum(d, 0.0) * wscaled[h:h + 1, :]
                bits = lax.bitcast_convert_type(acc, _I32)
                u = bits ^ ((bits >> 31) | INT_MIN)
                if diagonal:
                    s_chunk = (r0 + lax.broadcasted_iota(_I32, (rows, tq), 0)) // CHUNK
                    t_chunk = lax.broadcasted_iota(_I32, (rows, tq), 1) // CHUNK
                    u = jnp.where(s_chunk <= t_chunk, u, 0)
                u_buf[pl.ds(off + c * rows, rows), :] = u
            for lg in range(nlg):
                blk = [u_buf[pl.ds(off + 8 * m, 8), lg * LANES:(lg + 1) * LANES]
                       for m in range(WORD_BITS)]
                planes = _bit_transpose(blk)
                for b in range(WORD_BITS):
                    p_sc[j, hf, lg, 8 * b:8 * b + 8, :] = planes[b]

    def fill(j, carry):
        planes_of_tile(j, diagonal=False)
        return carry

    @pl.when((pl.program_id(0) == 0) & (i == 0))
    def _():
        def zero_tile(j, carry):
            p_sc[j] = jnp.zeros(p_sc.shape[1:], _I32)
            return carry
        lax.fori_loop(0, p_sc.shape[0], zero_tile, 0)

    lax.fori_loop(0, i, fill, 0)
    planes_of_tile(i, diagonal=True)

    ntiles = p_sc.shape[0]
    ngroups = lax.div(nj + RADIX_GROUP - 1, RADIX_GROUP)

    def init(j, carry):
        c_sc[j] = jnp.full(c_sc.shape[1:], jnp.where(j < nj, -1, 0), _I32)
        g_sc[j] = jnp.zeros(g_sc.shape[1:], _I32)
        return carry

    lax.fori_loop(0, jnp.minimum(ngroups * RADIX_GROUP, ntiles), init, 0)

    def per_query_sum(parts):
        return [jnp.broadcast_to(jnp.sum(p, axis=0, keepdims=True), (8, LANES)) for p in parts]

    def count(word_fn):
        def body(j, acc):
            acc = list(acc)
            for hf in range(2):
                for lg in range(nlg):
                    acc[lg] = acc[lg] + lax.population_count(word_fn(j, hf, lg))
            return tuple(acc)
        zero = jnp.zeros((8, LANES), _I32)
        return per_query_sum(lax.fori_loop(0, nj, body, (zero,) * nlg))

    def radix_step(step, carry):
        need, found, ones = carry
        b = WORD_BITS - 1 - step
        b8 = pl.multiple_of(b * 8, 8)
        next8 = pl.multiple_of(jnp.maximum(b - 1, 0) * 8, 8)
        take = [ones[lg] >= need[lg] for lg in range(nlg)]

        def sweep(grp, acc):
            acc = list(acc)
            for jj in range(RADIX_GROUP):
                j = grp * RADIX_GROUP + jj
                for hf in range(2):
                    for lg in range(nlg):
                        cand = c_sc[j, hf, lg]
                        hit = cand & p_sc[j, hf, lg, pl.ds(b8, 8), :]
                        keep = jnp.where(take[lg], hit, cand ^ hit)
                        c_sc[j, hf, lg] = keep
                        g_sc[j, hf, lg] = g_sc[j, hf, lg] | jnp.where(take[lg], 0, hit)
                        acc[lg] = acc[lg] + lax.population_count(
                            keep & p_sc[j, hf, lg, pl.ds(next8, 8), :])
            return tuple(acc)

        zero = jnp.zeros((8, LANES), _I32)
        nxt = per_query_sum(lax.fori_loop(0, ngroups, sweep, (zero,) * nlg))
        need = tuple(jnp.where(take[lg], need[lg], need[lg] - ones[lg]) for lg in range(nlg))
        found = tuple(jnp.where(take[lg], 1, found[lg]) for lg in range(nlg))
        return need, found, tuple(nxt)

    need0 = (jnp.full((8, LANES), topk, _I32),) * nlg
    found0 = (jnp.zeros((8, LANES), _I32),) * nlg
    top8 = (WORD_BITS - 1) * 8
    ones0 = tuple(count(lambda j, hf, lg: p_sc[j, hf, lg, top8:top8 + 8, :]))
    need, found, _ = lax.fori_loop(0, WORD_BITS, radix_step, (need0, found0, ones0))
    found = [f != 0 for f in found]

    n_ties = count(lambda j, hf, lg: c_sc[j, hf, lg])
    straddle = [found[lg] & (n_ties[lg] > need[lg]) for lg in range(nlg)]
    any_straddle = functools.reduce(
        jnp.maximum, [jnp.max(jnp.where(s, 1.0, 0.0)) for s in straddle]) > 0.0
    sub = lax.broadcasted_iota(_I32, (8, LANES), 0)

    def upto(limit, j, hf):
        n = jnp.clip(((limit - (j * KEY_TILE + hf * half + sub)) >> 3) + 1, 0, WORD_BITS)
        return jnp.where(n >= WORD_BITS, -1, (jnp.int32(1) << jnp.minimum(n, WORD_BITS - 1)) - 1)

    def tie_limit():
        nbits = max(1, int(math.ceil(math.log2(p_sc.shape[0] * KEY_TILE))))

        def step(t, limit):
            cand = [limit[lg] | (jnp.int32(1) << (jnp.int32(nbits - 1) - t)) for lg in range(nlg)]
            before = count(lambda j, hf, lg: c_sc[j, hf, lg] & upto(cand[lg] - 1, j, hf))
            return tuple(jnp.where(before[lg] < need[lg], cand[lg], limit[lg])
                         for lg in range(nlg))

        limit = lax.fori_loop(0, nbits, step, (jnp.zeros((8, LANES), _I32),) * nlg)
        return tuple(jnp.where(straddle[lg], limit[lg], INT_MAX) for lg in range(nlg))

    limit = lax.cond(any_straddle, tie_limit,
                     lambda: (jnp.full((8, LANES), INT_MAX, _I32),) * nlg)

    def emit(j, carry):
        for hf in range(2):
            for lg in range(nlg):
                ties = jnp.where(found[lg], c_sc[j, hf, lg] & upto(limit[lg], j, hf), 0)
                out_ref[lg, j, hf] = g_sc[j, hf, lg] | ties
        return carry

    lax.fori_loop(0, nj, emit, 0)

    def clear(j, carry):
        for lg in range(nlg):
            out_ref[lg, j] = jnp.zeros((2, 8, LANES), _I32)
        return carry

    lax.fori_loop(nj, p_sc.shape[0], clear, 0)


def _indexer(ki, qi_t, wi_t, tq, topk):
    bsz, seq, _ = ki.shape
    nt = seq // KEY_TILE
    nlg = tq // LANES
    assert seq % KEY_TILE == 0 and tq == KEY_TILE and KEY_TILE % (2 * 8 * WORD_BITS) == 0
    assert nt % RADIX_GROUP == 0
    return pl.pallas_call(
        functools.partial(_indexer_kernel, tq=tq, topk=topk),
        grid=(bsz, seq // tq),
        in_specs=[
            pl.BlockSpec((None, seq, IDX_DIM), lambda b, i: (b, 0, 0),
                         pipeline_mode=pl.Buffered(1)),
            pl.BlockSpec((None, IDX_HEADS, IDX_DIM, tq), lambda b, i: (b, 0, 0, i)),
            pl.BlockSpec((None, IDX_HEADS, tq), lambda b, i: (b, 0, i)),
        ],
        out_specs=pl.BlockSpec((None, nlg, nt, 2, 8, LANES), lambda b, i: (b, i, 0, 0, 0, 0)),
        out_shape=jax.ShapeDtypeStruct((bsz, seq // LANES, nt, 2, 8, LANES), _I32),
        scratch_shapes=[
            pltpu.VMEM((nt, 2, nlg, KEY_TILE // 2, LANES), _I32),
            pltpu.VMEM((nt, 2, nlg, 8, LANES), _I32),
            pltpu.VMEM((nt, 2, nlg, 8, LANES), _I32),
            pltpu.VMEM((KEY_TILE // 2 + 8, tq), _I32),
            pltpu.VMEM((KEY_TILE // 2 + 8, tq), _I32),
        ],
        compiler_params=pltpu.CompilerParams(
            dimension_semantics=("arbitrary", "arbitrary"), vmem_limit_bytes=VMEM_LIMIT),
        name="dsa_indexer_topk",
    )(ki, qi_t, wi_t)


def _attn_kernel(bt_ref, it_ref, jt_ref, q_ref, k_ref, vt_ref, words_ref, t0_ref, t1_ref,
                 z_ref, x_ref, wo_ref, o_ref, m_sc, l_sc, acc_sc, mask_sc, s0_sc, s1_sc, p0_sc, p1_sc,
                 *, tq, tk):
    step = pl.program_id(0)
    i = it_ref[step]
    j = jt_ref[step]
    nsub = tq // LANES

    @pl.when(j == 0)
    def _():
        m_sc[...] = jnp.full(m_sc.shape, NEG_BIG, _F32)
        l_sc[...] = jnp.zeros(l_sc.shape, _F32)
        acc_sc[...] = jnp.zeros(acc_sc.shape, _F32)

    def build_mask():
        for hf in range(2):
            for lg in range(tq // LANES):
                w = words_ref[lg, hf]
                for m in range(WORD_BITS):
                    r0 = hf * (tk // 2) + 8 * m
                    mask_sc[r0:r0 + 8, lg * LANES:(lg + 1) * LANES] = jnp.where(
                        (w << (WORD_BITS - 1 - m)) < 0, 0.0, -jnp.inf)

    rows = 64
    s_bufs = (s0_sc, s1_sc)
    p_bufs = (p0_sc, p1_sc)
    off = pl.multiple_of(jnp.minimum(j, 0) * STAGE_PAD, STAGE_PAD)

    def fold_rows(x, op):
        return op(x.reshape(x.shape[0] // 8, 8, tq), axis=0)

    def bias_block(h, kind, key_blk, qry_blk, r):
        if kind == "diag" and key_blk == qry_blk:
            return t0_ref[h, r:r + rows, :]
        if kind == "diag" and key_blk + 1 == qry_blk:
            return t1_ref[h, r:r + rows, :]
        if kind == "sub" and key_blk == nsub - 1 and qry_blk == 0:
            return t1_ref[h, r:r + rows, :]
        return None

    def logits(h, kind):
        hs = slice(h * A_HEAD_DIM, (h + 1) * A_HEAD_DIM)
        s_buf = s_bufs[h % 2]
        s = lax.dot_general(k_ref[:, hs], q_ref[:, hs], (((1,), (1,)), ((), ())),
                            preferred_element_type=_F32)
        m8 = None
        for r0 in range(0, tk, rows):
            t = s[r0:r0 + rows, :] + mask_sc[r0:r0 + rows, :]
            if kind != "far":
                parts = []
                for c in range(nsub):
                    blk = t[:, c * LANES:(c + 1) * LANES]
                    bias = bias_block(h, kind, r0 // LANES, c, r0 % LANES)
                    parts.append(blk if bias is None else blk + bias)
                t = jnp.concatenate(parts, axis=1)
            s_buf[pl.ds(off + r0, rows), :] = t
            m8 = fold_rows(t, jnp.max) if m8 is None else jnp.maximum(m8, fold_rows(t, jnp.max))
        return m8

    def accumulate(h, m8):
        hs = slice(h * A_HEAD_DIM, (h + 1) * A_HEAD_DIM)
        s_buf, p_buf = s_bufs[h % 2], p_bufs[h % 2]
        m_prev = m_sc[h:h + 1, :]
        m_new = jnp.maximum(m_prev, jnp.max(m8, axis=0, keepdims=True))
        alpha = jnp.exp2(m_prev - m_new)
        for r0 in range(0, tk, rows):
            p = jnp.exp2(s_buf[pl.ds(off + r0, rows), :] - m_new)
            p_buf[pl.ds(off + r0, rows), :] = p.astype(_BF16)
        pv = jnp.dot(vt_ref[h], p_buf[pl.ds(off, tk), :], preferred_element_type=_F32)
        acc_sc[hs, :] = alpha * acc_sc[hs, :] + pv[0:A_HEAD_DIM, :]
        l_sc[h:h + 1, :] = alpha * l_sc[h:h + 1, :] + pv[A_HEAD_DIM:A_HEAD_DIM + 1, :]
        m_sc[h:h + 1, :] = m_new

    def all_heads(kind):
        build_mask()
        m8 = logits(0, kind)
        for h in range(A_HEADS):
            m8_next = logits(h + 1, kind) if h + 1 < A_HEADS else None
            accumulate(h, m8)
            m8 = m8_next

    @pl.when(j < i - 1)
    def _():
        all_heads("far")

    @pl.when(j == i - 1)
    def _():
        all_heads("sub")

    @pl.when(j == i)
    def _():
        all_heads("diag")

    @pl.when(j == i)
    def _():
        parts = []
        for h in range(A_HEADS):
            hs = slice(h * A_HEAD_DIM, (h + 1) * A_HEAD_DIM)
            out_h = jnp.transpose(acc_sc[hs, :] / l_sc[h:h + 1, :])
            parts.append((out_h * _silu(z_ref[:, hs])).astype(_BF16))
        gated = jnp.concatenate(parts, axis=1)
        o_ref[...] = x_ref[...] + jnp.dot(gated, wo_ref[...], preferred_element_type=_F32)


def _attention(qk, v_t, words_t, t0, t1, z, x, wo, tq):
    bsz, seq, d = x.shape
    tk = KEY_TILE
    assert tq == tk and seq % tq == 0
    nq = seq // tq
    bt, it, jt = [], [], []
    for b in range(bsz):
        for i in range(nq):
            for j in range(i + 1):
                bt.append(b)
                it.append(i)
                jt.append(j)
    tables = [jnp.asarray(np.asarray(t, np.int32)) for t in (bt, it, jt)]

    grid_spec = pltpu.PrefetchScalarGridSpec(
        num_scalar_prefetch=3,
        grid=(len(bt),),
        in_specs=[
            pl.BlockSpec((None, tq, A_WIDTH), lambda s, b, i, j: (b[s], i[s], 0)),
            pl.BlockSpec((None, tk, A_WIDTH), lambda s, b, i, j: (b[s], j[s], 1)),
            pl.BlockSpec((None, A_HEADS, A_HEAD_DIM + STAGE_PAD, tk),
                         lambda s, b, i, j: (b[s], 0, 0, j[s])),
            pl.BlockSpec((None, tq // LANES, None, 2, 8, LANES),
                         lambda s, b, i, j: (b[s], i[s], j[s], 0, 0, 0)),
            _const_spec(t0.shape),
            _const_spec(t1.shape),
            pl.BlockSpec((None, tq, A_WIDTH), lambda s, b, i, j: (b[s], i[s], 0)),
            pl.BlockSpec((None, tq, d), lambda s, b, i, j: (b[s], i[s], 0)),
            _const_spec(wo.shape),
        ],
        out_specs=pl.BlockSpec((None, tq, d), lambda s, b, i, j: (b[s], i[s], 0)),
        scratch_shapes=[
            pltpu.VMEM((A_HEADS, tq), _F32),
            pltpu.VMEM((A_HEADS, tq), _F32),
            pltpu.VMEM((A_WIDTH, tq), _F32),
            pltpu.VMEM((tk, tq), _F32),
            pltpu.VMEM((tk + STAGE_PAD, tq), _F32),
            pltpu.VMEM((tk + STAGE_PAD, tq), _F32),
            pltpu.VMEM((tk + STAGE_PAD, tq), _BF16),
            pltpu.VMEM((tk + STAGE_PAD, tq), _BF16),
        ],
    )
    return pl.pallas_call(
        functools.partial(_attn_kernel, tq=tq, tk=tk),
        grid_spec=grid_spec,
        out_shape=jax.ShapeDtypeStruct((bsz, seq, d), _F32),
        compiler_params=pltpu.CompilerParams(
            dimension_semantics=("arbitrary",), vmem_limit_bytes=VMEM_LIMIT),
        name="dsa_attention",
    )(*tables, qk, qk, v_t, words_t, t0, t1, z, x, wo)


def _sgu_kernel(x_ref, g_ref, win_ref, lng_ref, lnb_ref, ws_ref, bst_ref, wout_ref, fg_ref,
                o_ref, *, tm, width):
    x = x_ref[...]
    hb = _rms_normalize(x, g_ref[...]).astype(_BF16)
    gdim = width // B_GROUPS

    def mm(lo):
        return jnp.dot(hb, win_ref[:, lo:lo + width], preferred_element_type=_F32)

    v = mm(width)
    mu = jnp.mean(v, axis=-1, keepdims=True)
    vc = v - mu
    var = jnp.mean(vc * vc, axis=-1, keepdims=True)
    vn = ((vc * lax.rsqrt(var + LN_EPS)) * lng_ref[...] + lnb_ref[...]).astype(_BF16)

    row = lax.broadcasted_iota(_I32, (SGU_CHUNK, SGU_CHUNK), 0)
    col = lax.broadcasted_iota(_I32, (SGU_CHUNK, SGU_CHUNK), 1)
    bst = bst_ref[...]
    cols = []
    for g in range(B_GROUPS):
        ws = jnp.where(row >= col, ws_ref[g], 0.0).astype(_BF16)
        bias = bst[:, g:g + 1]
        chunks = []
        for c in range(tm // SGU_CHUNK):
            blk = vn[c * SGU_CHUNK:(c + 1) * SGU_CHUNK, g * gdim:(g + 1) * gdim]
            chunks.append(jnp.dot(ws, blk, preferred_element_type=_F32) + bias)
        cols.append(jnp.concatenate(chunks, axis=0))
    mixed = jnp.concatenate(cols, axis=1)

    y = (mm(0) * mixed * _silu(mm(2 * width))).astype(_BF16)
    x2 = x + jnp.dot(y, wout_ref[...], preferred_element_type=_F32)
    o_ref[...] = _rms_normalize(x2, fg_ref[...])


def _sgu_layer(x2, g, w_in, ln_g, ln_b, w_s, b_s_t, w_out, final_g, tm):
    m, d = x2.shape
    width = w_out.shape[0]
    return pl.pallas_call(
        functools.partial(_sgu_kernel, tm=tm, width=width),
        grid=(m // tm,),
        in_specs=[
            pl.BlockSpec((tm, d), lambda i: (i, 0)),
            _const_spec((1, d)),
            _const_spec(w_in.shape),
            _const_spec((1, width)),
            _const_spec((1, width)),
            _const_spec(w_s.shape),
            _const_spec(b_s_t.shape),
            _const_spec(w_out.shape),
            _const_spec((1, d)),
        ],
        out_specs=pl.BlockSpec((tm, d), lambda i: (i, 0)),
        out_shape=jax.ShapeDtypeStruct((m, d), _F32),
        compiler_params=pltpu.CompilerParams(
            dimension_semantics=("arbitrary",), vmem_limit_bytes=VMEM_LIMIT),
        name="sgu_layer_final_norm",
    )(x2, g, w_in, ln_g, ln_b, w_s, b_s_t, w_out, final_g)


def _t5_bucket(rel):
    half = REL_BUCKETS // 2
    max_exact = half // 2
    ret = jnp.where(rel < 0, half, 0)
    n = jnp.abs(rel)
    nf = jnp.maximum(n, 1).astype(jnp.float32)
    large = max_exact + (jnp.log(nf / max_exact) / math.log(REL_MAX_DIST / max_exact)
                         * (half - max_exact)).astype(jnp.int32)
    large = jnp.minimum(large, half - 1)
    return ret + jnp.where(n < max_exact, n, large)


def _bias_tables(rel_bias):
    n = LANES
    far = rel_bias[_t5_bucket(jnp.int32(4 * REL_MAX_DIST))]

    def toeplitz(first_rel):
        rel = first_rel + jnp.arange(2 * n - 1, dtype=jnp.int32)
        vals = jnp.transpose((rel_bias[_t5_bucket(rel)] - far) * LOG2_E)
        hankel = jnp.tile(vals, (1, n + 1))[:, :n * 2 * n].reshape(-1, n, 2 * n)[:, :, :n]
        return hankel[:, ::-1, :]

    return toeplitz(-(n - 1)), toeplitz(1)


def kernel(x, norm_g, final_g, rel_bias, a_w_in, a_w_out, b_w_in, b_ln_g, b_ln_b, b_w_s,
           b_b_s, b_w_out):
    bsz, seq, d = x.shape
    assert norm_g.shape[0] == 2 and a_w_in.shape[0] == 1 and b_w_in.shape[0] == 1
    m = bsz * seq
    n_qi = IDX_HEADS * IDX_DIM
    topk = min(TOPK_MAX, seq // 4)

    cols = a_w_in.shape[2]
    w_pad = jnp.pad(a_w_in[0], ((0, 0), (0, 4 * A_WIDTH + n_qi + LANES - cols))).astype(_BF16)
    qk, v_t, z, qi_t, ki, wi_t = _proj_a(x, norm_g[0:1], w_pad, tm=512)
    words = _indexer(ki, qi_t, wi_t, tq=KEY_TILE, topk=topk)
    t0, t1 = _bias_tables(rel_bias)
    x1 = _attention(qk, v_t, words, t0, t1, z, x, a_w_out[0].astype(_BF16), tq=KEY_TILE)

    out = _sgu_layer(x1.reshape(m, d), norm_g[1:2], b_w_in[0].astype(_BF16),
                     b_ln_g[0:1], b_ln_b[0:1], b_w_s[0], jnp.transpose(b_b_s[0]),
                     b_w_out[0].astype(_BF16), final_g.reshape(1, d), tm=256)
    return out.reshape(bsz, seq, d)
```

```python
import functools
import math

import numpy as np
import jax
import jax.numpy as jnp
from jax import lax
from jax.experimental import pallas as pl
from jax.experimental.pallas import tpu as pltpu

CHUNK = 64
RMS_EPS = 1e-6
LN_EPS = 1e-5

A_HEADS = 8
A_HEAD_DIM = 128
A_WIDTH = A_HEADS * A_HEAD_DIM
IDX_HEADS = 8
IDX_DIM = 64
TOPK_MAX = 256

REL_BUCKETS = 32
REL_MAX_DIST = 128

SGU_CHUNK = 128
B_GROUPS = 8

LANES = 128
KEY_TILE = 512
WORD_BITS = 32
RADIX_GROUP = 4
STAGE_PAD = 16
VMEM_LIMIT = 56 * 1024 * 1024

LOG2_E = math.log2(math.e)
Q_SCALE = A_HEAD_DIM ** -0.5 * LOG2_E

INT_MIN = -2 ** 31
INT_MAX = 2 ** 31 - 1
NEG_BIG = -1e30

_BF16 = jnp.bfloat16
_F32 = jnp.float32
_I32 = jnp.int32


def _const_spec(shape):
    zeros = (0,) * len(shape)
    return pl.BlockSpec(shape, lambda *_: zeros, pipeline_mode=pl.Buffered(1))


def _rms_normalize(x, g):
    return x * lax.rsqrt(jnp.mean(x * x, axis=-1, keepdims=True) + RMS_EPS) * g


def _silu(z):
    return z / (1.0 + jnp.exp(-z))


def _proj_a_kernel(x_ref, g_ref, w_ref, qk_ref, vt_ref, z_ref, qit_ref, ki_ref, wit_ref):
    hb = _rms_normalize(x_ref[...], g_ref[...]).astype(_BF16)
    tm = hb.shape[0]

    def mm(lo, width):
        return jnp.dot(hb, w_ref[:, lo:lo + width], preferred_element_type=_F32)

    step = 512
    for c in range(0, 2 * A_WIDTH, step):
        scale = Q_SCALE if c < A_WIDTH else 1.0
        qk_ref[:, c:c + step] = (mm(c, step) * scale).astype(_BF16)
    pad_rows = lax.broadcasted_iota(_I32, (STAGE_PAD, tm), 0)
    for c in range(0, A_WIDTH, step):
        v_c = mm(2 * A_WIDTH + c, step)
        for hh in range(step // A_HEAD_DIM):
            h = c // A_HEAD_DIM + hh
            v_h = v_c[:, hh * A_HEAD_DIM:(hh + 1) * A_HEAD_DIM]
            vt_ref[h, 0:A_HEAD_DIM, :] = jnp.transpose(v_h).astype(_BF16)
            vt_ref[h, A_HEAD_DIM:, :] = jnp.where(pad_rows == 0, 1.0, 0.0).astype(_BF16)
    for c in range(0, A_WIDTH, step):
        z_ref[:, c:c + step] = mm(3 * A_WIDTH + c, step)
    qi_t = jnp.transpose(mm(4 * A_WIDTH, IDX_HEADS * IDX_DIM))
    for h in range(IDX_HEADS):
        qit_ref[h] = qi_t[h * IDX_DIM:(h + 1) * IDX_DIM, :].astype(_BF16)
    tail = mm(4 * A_WIDTH + IDX_HEADS * IDX_DIM, LANES)
    ki_ref[...] = tail[:, 0:IDX_DIM].astype(_BF16)
    wit_ref[...] = jnp.transpose(tail)[IDX_DIM:IDX_DIM + IDX_HEADS, :]


def _proj_a(x, g, w_pad, tm):
    bsz, seq, d = x.shape
    row = lambda b, i: (b, i, 0)
    return pl.pallas_call(
        _proj_a_kernel,
        grid=(bsz, seq // tm),
        in_specs=[
            pl.BlockSpec((None, tm, d), row),
            _const_spec((1, d)),
            _const_spec(w_pad.shape),
        ],
        out_specs=[
            pl.BlockSpec((None, tm, 2 * A_WIDTH), row),
            pl.BlockSpec((None, A_HEADS, A_HEAD_DIM + STAGE_PAD, tm), lambda b, i: (b, 0, 0, i)),
            pl.BlockSpec((None, tm, A_WIDTH), row),
            pl.BlockSpec((None, IDX_HEADS, IDX_DIM, tm), lambda b, i: (b, 0, 0, i)),
            pl.BlockSpec((None, tm, IDX_DIM), row),
            pl.BlockSpec((None, IDX_HEADS, tm), lambda b, i: (b, 0, i)),
        ],
        out_shape=[
            jax.ShapeDtypeStruct((bsz, seq, 2 * A_WIDTH), _BF16),
            jax.ShapeDtypeStruct((bsz, A_HEADS, A_HEAD_DIM + STAGE_PAD, seq), _BF16),
            jax.ShapeDtypeStruct((bsz, seq, A_WIDTH), _F32),
            jax.ShapeDtypeStruct((bsz, IDX_HEADS, IDX_DIM, seq), _BF16),
            jax.ShapeDtypeStruct((bsz, seq, IDX_DIM), _BF16),
            jax.ShapeDtypeStruct((bsz, IDX_HEADS, seq), _F32),
        ],
        compiler_params=pltpu.CompilerParams(
            dimension_semantics=("arbitrary", "arbitrary"), vmem_limit_bytes=VMEM_LIMIT),
        name="dsa_in_proj",
    )(x, g, w_pad)


def _bit_transpose(vregs):
    a = list(reversed(vregs))
    j, mask = 16, 0x0000FFFF
    while j:
        k = 0
        while k < WORD_BITS:
            t = (a[k] ^ lax.shift_right_logical(a[k + j], jnp.int32(j))) & mask
            a[k] = a[k] ^ t
            a[k + j] = a[k + j] ^ (t << j)
            k = (k + j + 1) & ~j
        j >>= 1
        mask = (mask ^ (mask << j)) & 0xFFFFFFFF
        if mask >= 2 ** 31:
            mask -= 2 ** 32
    return list(reversed(a))


def _indexer_kernel(ki_ref, qit_ref, wit_ref, out_ref, p_sc, c_sc, g_sc, u0_sc, u1_sc,
                    *, tq, topk):
    i = pl.program_id(1)
    nlg = tq // LANES
    half = KEY_TILE // 2
    nj = i + 1
    wscaled = wit_ref[...] * (IDX_HEADS ** -0.5 * IDX_DIM ** -0.5)

    rows = 64
    u_bufs = (u0_sc, u1_sc)
    off = pl.multiple_of(jnp.minimum(i, 0) * 8, 8)

    def planes_of_tile(j, diagonal):
        base = pl.multiple_of(j * KEY_TILE, KEY_TILE)
        for hf in range(2):
            u_buf = u_bufs[hf]
            for c in range(half // rows):
                r0 = hf * half + c * rows
                kt = ki_ref[pl.ds(base + r0, rows), :]
                acc = jnp.zeros((rows, tq), _F32)
                for h in range(IDX_HEADS):
                    d = jnp.dot(kt, qit_ref[h], preferred_element_type=_F32)
                    acc = acc + jnp.maximum(d, 0.0) * wscaled[h:h + 1, :]
                bits = lax.bitcast_convert_type(acc, _I32)
                u = bits ^ ((bits >> 31) | INT_MIN)
                if diagonal:
                    s_chunk = (r0 + lax.broadcasted_iota(_I32, (rows, tq), 0)) // CHUNK
                    t_chunk = lax.broadcasted_iota(_I32, (rows, tq), 1) // CHUNK
                    u = jnp.where(s_chunk <= t_chunk, u, 0)
                u_buf[pl.ds(off + c * rows, rows), :] = u
            for lg in range(nlg):
                blk = [u_buf[pl.ds(off + 8 * m, 8), lg * LANES:(lg + 1) * LANES]
                       for m in range(WORD_BITS)]
                planes = _bit_transpose(blk)
                for b in range(WORD_BITS):
                    p_sc[j, hf, lg, 8 * b:8 * b + 8, :] = planes[b]

    def fill(j, carry):
        planes_of_tile(j, diagonal=False)
        return carry

    @pl.when((pl.program_id(0) == 0) & (i == 0))
    def _():
        def zero_tile(j, carry):
            p_sc[j] = jnp.zeros(p_sc.shape[1:], _I32)
            return carry
        lax.fori_loop(0, p_sc.shape[0], zero_tile, 0)

    lax.fori_loop(0, i, fill, 0)
    planes_of_tile(i, diagonal=True)

    ntiles = p_sc.shape[0]
    ngroups = lax.div(nj + RADIX_GROUP - 1, RADIX_GROUP)

    def init(j, carry):
        c_sc[j] = jnp.full(c_sc.shape[1:], jnp.where(j < nj, -1, 0), _I32)
        g_sc[j] = jnp.zeros(g_sc.shape[1:], _I32)
        return carry

    lax.fori_loop(0, jnp.minimum(ngroups * RADIX_GROUP, ntiles), init, 0)

    def per_query_sum(parts):
        return [jnp.broadcast_to(jnp.sum(p, axis=0, keepdims=True), (8, LANES)) for p in parts]

    def count(word_fn):
        def body(j, acc):
            acc = list(acc)
            for hf in range(2):
                for lg in range(nlg):
                    acc[lg] = acc[lg] + lax.population_count(word_fn(j, hf, lg))
            return tuple(acc)
        zero = jnp.zeros((8, LANES), _I32)
        return per_query_sum(lax.fori_loop(0, nj, body, (zero,) * nlg))

    def radix_step(step, carry):
        need, found, ones = carry
        b = WORD_BITS - 1 - step
        b8 = pl.multiple_of(b * 8, 8)
        next8 = pl.multiple_of(jnp.maximum(b - 1, 0) * 8, 8)
        take = [ones[lg] >= need[lg] for lg in range(nlg)]

        def sweep(grp, acc):
            acc = list(acc)
            for jj in range(RADIX_GROUP):
                j = grp * RADIX_GROUP + jj
                for hf in range(2):
                    for lg in range(nlg):
                        cand = c_sc[j, hf, lg]
                        hit = cand & p_sc[j, hf, lg, pl.ds(b8, 8), :]
                        keep = jnp.where(take[lg], hit, cand ^ hit)
                        c_sc[j, hf, lg] = keep
                        g_sc[j, hf, lg] = g_sc[j, hf, lg] | jnp.where(take[lg], 0, hit)
                        acc[lg] = acc[lg] + lax.population_count(
                            keep & p_sc[j, hf, lg, pl.ds(next8, 8), :])
            return tuple(acc)

        zero = jnp.zeros((8, LANES), _I32)
        nxt = per_query_sum(lax.fori_loop(0, ngroups, sweep, (zero,) * nlg))
        need = tuple(jnp.where(take[lg], need[lg], need[lg] - ones[lg]) for lg in range(nlg))
        found = tuple(jnp.where(take[lg], 1, found[lg]) for lg in range(nlg))
        return need, found, tuple(nxt)

    need0 = (jnp.full((8, LANES), topk, _I32),) * nlg
    found0 = (jnp.zeros((8, LANES), _I32),) * nlg
    top8 = (WORD_BITS - 1) * 8
    ones0 = tuple(count(lambda j, hf, lg: p_sc[j, hf, lg, top8:top8 + 8, :]))
    need, found, _ = lax.fori_loop(0, WORD_BITS, radix_step, (need0, found0, ones0))
    found = [f != 0 for f in found]

    n_ties = count(lambda j, hf, lg: c_sc[j, hf, lg])
    straddle = [found[lg] & (n_ties[lg] > need[lg]) for lg in range(nlg)]
    any_straddle = functools.reduce(
        jnp.maximum, [jnp.max(jnp.where(s, 1.0, 0.0)) for s in straddle]) > 0.0
    sub = lax.broadcasted_iota(_I32, (8, LANES), 0)

    def upto(limit, j, hf):
        n = jnp.clip(((limit - (j * KEY_TILE + hf * half + sub)) >> 3) + 1, 0, WORD_BITS)
        return jnp.where(n >= WORD_BITS, -1, (jnp.int32(1) << jnp.minimum(n, WORD_BITS - 1)) - 1)

    def tie_limit():
        nbits = max(1, int(math.ceil(math.log2(p_sc.shape[0] * KEY_TILE))))

        def step(t, limit):
            cand = [limit[lg] | (jnp.int32(1) << (jnp.int32(nbits - 1) - t)) for lg in range(nlg)]
            before = count(lambda j, hf, lg: c_sc[j, hf, lg] & upto(cand[lg] - 1, j, hf))
            return tuple(jnp.where(before[lg] < need[lg], cand[lg], limit[lg])
                         for lg in range(nlg))

        limit = lax.fori_loop(0, nbits, step, (jnp.zeros((8, LANES), _I32),) * nlg)
        return tuple(jnp.where(straddle[lg], limit[lg], INT_MAX) for lg in range(nlg))

    limit = lax.cond(any_straddle, tie_limit,
                     lambda: (jnp.full((8, LANES), INT_MAX, _I32),) * nlg)

    def emit(j, carry):
        for hf in range(2):
            for lg in range(nlg):
                ties = jnp.where(found[lg], c_sc[j, hf, lg] & upto(limit[lg], j, hf), 0)
                out_ref[lg, j, hf] = g_sc[j, hf, lg] | ties
        return carry

    lax.fori_loop(0, nj, emit, 0)

    def clear(j, carry):
        for lg in range(nlg):
            out_ref[lg, j] = jnp.zeros((2, 8, LANES), _I32)
        return carry

    lax.fori_loop(nj, p_sc.shape[0], clear, 0)


def _indexer(ki, qi_t, wi_t, tq, topk):
    bsz, seq, _ = ki.shape
    nt = seq // KEY_TILE
    nlg = tq // LANES
    assert seq % KEY_TILE == 0 and tq == KEY_TILE and KEY_TILE % (2 * 8 * WORD_BITS) == 0
    assert nt % RADIX_GROUP == 0
    return pl.pallas_call(
        functools.partial(_indexer_kernel, tq=tq, topk=topk),
        grid=(bsz, seq // tq),
        in_specs=[
            pl.BlockSpec((None, seq, IDX_DIM), lambda b, i: (b, 0, 0),
                         pipeline_mode=pl.Buffered(1)),
            pl.BlockSpec((None, IDX_HEADS, IDX_DIM, tq), lambda b, i: (b, 0, 0, i)),
            pl.BlockSpec((None, IDX_HEADS, tq), lambda b, i: (b, 0, i)),
        ],
        out_specs=pl.BlockSpec((None, nlg, nt, 2, 8, LANES), lambda b, i: (b, i, 0, 0, 0, 0)),
        out_shape=jax.ShapeDtypeStruct((bsz, seq // LANES, nt, 2, 8, LANES), _I32),
        scratch_shapes=[
            pltpu.VMEM((nt, 2, nlg, KEY_TILE // 2, LANES), _I32),
            pltpu.VMEM((nt, 2, nlg, 8, LANES), _I32),
            pltpu.VMEM((nt, 2, nlg, 8, LANES), _I32),
            pltpu.VMEM((KEY_TILE // 2 + 8, tq), _I32),
            pltpu.VMEM((KEY_TILE // 2 + 8, tq), _I32),
        ],
        compiler_params=pltpu.CompilerParams(
            dimension_semantics=("arbitrary", "arbitrary"), vmem_limit_bytes=VMEM_LIMIT),
        name="dsa_indexer_topk",
    )(ki, qi_t, wi_t)


def _attn_kernel(bt_ref, it_ref, jt_ref, jbt_ref, q_ref, ka_ref, kb_ref, vta_ref, vtb_ref, wa_ref, wb_ref,
                 t0_ref, t1_ref, z_ref, x_ref, wo_ref, o_ref, m_sc, l_sc, acc_sc, mask_sc,
                 s0_sc, s1_sc, p0_sc, p1_sc, *, tq, tk):
    step = pl.program_id(0)
    i = it_ref[step]
    j = jt_ref[step]
    nsub = tq // LANES
    k_refs, vt_refs, words_refs = (ka_ref, kb_ref), (vta_ref, vtb_ref), (wa_ref, wb_ref)

    @pl.when(j == 0)
    def _():
        m_sc[...] = jnp.full(m_sc.shape, NEG_BIG, _F32)
        l_sc[...] = jnp.zeros(l_sc.shape, _F32)
        acc_sc[...] = jnp.zeros(acc_sc.shape, _F32)

    def build_mask(tile):
        for hf in range(2):
            for lg in range(tq // LANES):
                w = words_refs[tile][lg, hf]
                for m in range(WORD_BITS):
                    r0 = hf * (tk // 2) + 8 * m
                    mask_sc[tile, r0:r0 + 8, lg * LANES:(lg + 1) * LANES] = jnp.where(
                        (w << (WORD_BITS - 1 - m)) < 0, 0.0, -jnp.inf)

    rows = 64
    s_bufs = (s0_sc, s1_sc)
    p_bufs = (p0_sc, p1_sc)
    off = pl.multiple_of(jnp.minimum(j, 0) * STAGE_PAD, STAGE_PAD)

    def fold_rows(x, op):
        return op(x.reshape(x.shape[0] // 8, 8, tq), axis=0)

    def bias_block(h, kind, key_blk, qry_blk, r):
        if kind == "diag" and key_blk == qry_blk:
            return t0_ref[h, r:r + rows, :]
        if kind == "diag" and key_blk + 1 == qry_blk:
            return t1_ref[h, r:r + rows, :]
        if kind == "sub" and key_blk == nsub - 1 and qry_blk == 0:
            return t1_ref[h, r:r + rows, :]
        return None

    def qk(unit):
        tile, h = unit
        hs = slice(h * A_HEAD_DIM, (h + 1) * A_HEAD_DIM)
        return lax.dot_general(k_refs[tile][:, hs], q_ref[:, hs], (((1,), (1,)), ((), ())),
                               preferred_element_type=_F32)

    def stage_logits(n, unit, kind, s, r0, m8):
        tile, h = unit
        t = s[r0:r0 + rows, :] + mask_sc[tile, r0:r0 + rows, :]
        if kind != "far":
            parts = []
            for c in range(nsub):
                blk = t[:, c * LANES:(c + 1) * LANES]
                bias = bias_block(h, kind, r0 // LANES, c, r0 % LANES)
                parts.append(blk if bias is None else blk + bias)
            t = jnp.concatenate(parts, axis=1)
        s_bufs[n % 2][pl.ds(off + r0, rows), :] = t
        return fold_rows(t, jnp.max) if m8 is None else jnp.maximum(m8, fold_rows(t, jnp.max))

    def run_tiles(kinds):
        units = [(tile, h) for tile in range(len(kinds)) for h in range(A_HEADS)]
        for tile in range(len(kinds)):
            build_mask(tile)
        s_next = qk(units[0])
        m8 = None
        for r0 in range(0, tk, rows):
            m8 = stage_logits(0, units[0], kinds[0], s_next, r0, m8)
        for n, (tile, h) in enumerate(units):
            hs = slice(h * A_HEAD_DIM, (h + 1) * A_HEAD_DIM)
            s_buf, p_buf = s_bufs[n % 2], p_bufs[n % 2]
            m_prev = m_sc[h:h + 1, :]
            m_new = jnp.maximum(m_prev, jnp.max(m8, axis=0, keepdims=True))
            alpha = jnp.exp2(m_prev - m_new)
            last = n + 1 == len(units)
            if not last:
                s_next = qk(units[n + 1])
            m8 = None
            for r0 in range(0, tk, rows):
                if not last:
                    m8 = stage_logits(n + 1, units[n + 1], kinds[units[n + 1][0]], s_next, r0, m8)
                p = jnp.exp2(s_buf[pl.ds(off + r0, rows), :] - m_new)
                p_buf[pl.ds(off + r0, rows), :] = p.astype(_BF16)
            pv = jnp.dot(vt_refs[tile][h], p_buf[pl.ds(off, tk), :], preferred_element_type=_F32)
            acc_sc[hs, :] = alpha * acc_sc[hs, :] + pv[0:A_HEAD_DIM, :]
            l_sc[h:h + 1, :] = alpha * l_sc[h:h + 1, :] + pv[A_HEAD_DIM:A_HEAD_DIM + 1, :]
            m_sc[h:h + 1, :] = m_new

    @pl.when(j + 1 < i - 1)
    def _():
        run_tiles(("far", "far"))

    @pl.when(j + 1 == i - 1)
    def _():
        run_tiles(("far", "sub"))

    @pl.when(j + 1 == i)
    def _():
        run_tiles(("sub", "diag"))

    @pl.when(j == i)
    def _():
        run_tiles(("diag",))

    @pl.when(j + 1 >= i)
    def _():
        parts = []
        for h in range(A_HEADS):
            hs = slice(h * A_HEAD_DIM, (h + 1) * A_HEAD_DIM)
            out_h = jnp.transpose(acc_sc[hs, :] / l_sc[h:h + 1, :])
            parts.append((out_h * _silu(z_ref[:, hs])).astype(_BF16))
        gated = jnp.concatenate(parts, axis=1)
        o_ref[...] = x_ref[...] + jnp.dot(gated, wo_ref[...], preferred_element_type=_F32)


def _attention(qk, v_t, words_t, t0, t1, z, x, wo, tq):
    bsz, seq, d = x.shape
    tk = KEY_TILE
    assert tq == tk and seq % tq == 0
    nq = seq // tq
    bt, it, jt, jbt = [], [], [], []
    for b in range(bsz):
        for i in range(nq):
            for j in range(0, i + 1, 2):
                bt.append(b)
                it.append(i)
                jt.append(j)
                jbt.append(min(j + 1, i))
    tables = [jnp.asarray(np.asarray(t, np.int32)) for t in (bt, it, jt, jbt)]
    vt_block = (None, A_HEADS, A_HEAD_DIM + STAGE_PAD, tk)
    words_block = (None, tq // LANES, None, 2, 8, LANES)

    grid_spec = pltpu.PrefetchScalarGridSpec(
        num_scalar_prefetch=4,
        grid=(len(bt),),
        in_specs=[
            pl.BlockSpec((None, tq, A_WIDTH), lambda s, b, i, j, jb: (b[s], i[s], 0)),
            pl.BlockSpec((None, tk, A_WIDTH), lambda s, b, i, j, jb: (b[s], j[s], 1)),
            pl.BlockSpec((None, tk, A_WIDTH), lambda s, b, i, j, jb: (b[s], jb[s], 1)),
            pl.BlockSpec(vt_block, lambda s, b, i, j, jb: (b[s], 0, 0, j[s])),
            pl.BlockSpec(vt_block, lambda s, b, i, j, jb: (b[s], 0, 0, jb[s])),
            pl.BlockSpec(words_block, lambda s, b, i, j, jb: (b[s], i[s], j[s], 0, 0, 0)),
            pl.BlockSpec(words_block, lambda s, b, i, j, jb: (b[s], i[s], jb[s], 0, 0, 0)),
            _const_spec(t0.shape),
            _const_spec(t1.shape),
            pl.BlockSpec((None, tq, A_WIDTH), lambda s, b, i, j, jb: (b[s], i[s], 0)),
            pl.BlockSpec((None, tq, d), lambda s, b, i, j, jb: (b[s], i[s], 0)),
            _const_spec(wo.shape),
        ],
        out_specs=pl.BlockSpec((None, tq, d), lambda s, b, i, j, jb: (b[s], i[s], 0)),
        scratch_shapes=[
            pltpu.VMEM((A_HEADS, tq), _F32),
            pltpu.VMEM((A_HEADS, tq), _F32),
            pltpu.VMEM((A_WIDTH, tq), _F32),
            pltpu.VMEM((2, tk, tq), _F32),
            pltpu.VMEM((tk + STAGE_PAD, tq), _F32),
            pltpu.VMEM((tk + STAGE_PAD, tq), _F32),
            pltpu.VMEM((tk + STAGE_PAD, tq), _BF16),
            pltpu.VMEM((tk + STAGE_PAD, tq), _BF16),
        ],
    )
    return pl.pallas_call(
        functools.partial(_attn_kernel, tq=tq, tk=tk),
        grid_spec=grid_spec,
        out_shape=jax.ShapeDtypeStruct((bsz, seq, d), _F32),
        compiler_params=pltpu.CompilerParams(
            dimension_semantics=("arbitrary",), vmem_limit_bytes=VMEM_LIMIT),
        name="dsa_attention",
    )(*tables, qk, qk, qk, v_t, v_t, words_t, words_t, t0, t1, z, x, wo)


def _sgu_kernel(x_ref, g_ref, win_ref, lng_ref, lnb_ref, ws_ref, bst_ref, wout_ref, fg_ref,
                o_ref, *, tm, width):
    x = x_ref[...]
    hb = _rms_normalize(x, g_ref[...]).astype(_BF16)
    gdim = width // B_GROUPS

    def mm(lo):
        return jnp.dot(hb, win_ref[:, lo:lo + width], preferred_element_type=_F32)

    v = mm(width)
    mu = jnp.mean(v, axis=-1, keepdims=True)
    vc = v - mu
    var = jnp.mean(vc * vc, axis=-1, keepdims=True)
    vn = ((vc * lax.rsqrt(var + LN_EPS)) * lng_ref[...] + lnb_ref[...]).astype(_BF16)

    row = lax.broadcasted_iota(_I32, (SGU_CHUNK, SGU_CHUNK), 0)
    col = lax.broadcasted_iota(_I32, (SGU_CHUNK, SGU_CHUNK), 1)
    bst = bst_ref[...]
    cols = []
    for g in range(B_GROUPS):
        ws = jnp.where(row >= col, ws_ref[g], 0.0).astype(_BF16)
        bias = bst[:, g:g + 1]
        chunks = []
        for c in range(tm // SGU_CHUNK):
            blk = vn[c * SGU_CHUNK:(c + 1) * SGU_CHUNK, g * gdim:(g + 1) * gdim]
            chunks.append(jnp.dot(ws, blk, preferred_element_type=_F32) + bias)
        cols.append(jnp.concatenate(chunks, axis=0))
    mixed = jnp.concatenate(cols, axis=1)

    y = (mm(0) * mixed * _silu(mm(2 * width))).astype(_BF16)
    x2 = x + jnp.dot(y, wout_ref[...], preferred_element_type=_F32)
    o_ref[...] = _rms_normalize(x2, fg_ref[...])


def _sgu_layer(x2, g, w_in, ln_g, ln_b, w_s, b_s_t, w_out, final_g, tm):
    m, d = x2.shape
    width = w_out.shape[0]
    return pl.pallas_call(
        functools.partial(_sgu_kernel, tm=tm, width=width),
        grid=(m // tm,),
        in_specs=[
            pl.BlockSpec((tm, d), lambda i: (i, 0)),
            _const_spec((1, d)),
            _const_spec(w_in.shape),
            _const_spec((1, width)),
            _const_spec((1, width)),
            _const_spec(w_s.shape),
            _const_spec(b_s_t.shape),
            _const_spec(w_out.shape),
            _const_spec((1, d)),
        ],
        out_specs=pl.BlockSpec((tm, d), lambda i: (i, 0)),
        out_shape=jax.ShapeDtypeStruct((m, d), _F32),
        compiler_params=pltpu.CompilerParams(
            dimension_semantics=("arbitrary",), vmem_limit_bytes=VMEM_LIMIT),
        name="sgu_layer_final_norm",
    )(x2, g, w_in, ln_g, ln_b, w_s, b_s_t, w_out, final_g)


def _t5_bucket(rel):
    half = REL_BUCKETS // 2
    max_exact = half // 2
    ret = jnp.where(rel < 0, half, 0)
    n = jnp.abs(rel)
    nf = jnp.maximum(n, 1).astype(jnp.float32)
    large = max_exact + (jnp.log(nf / max_exact) / math.log(REL_MAX_DIST / max_exact)
                         * (half - max_exact)).astype(jnp.int32)
    large = jnp.minimum(large, half - 1)
    return ret + jnp.where(n < max_exact, n, large)


def _bias_tables(rel_bias):
    n = LANES
    far = rel_bias[_t5_bucket(jnp.int32(4 * REL_MAX_DIST))]

    def toeplitz(first_rel):
        rel = first_rel + jnp.arange(2 * n - 1, dtype=jnp.int32)
        vals = jnp.transpose((rel_bias[_t5_bucket(rel)] - far) * LOG2_E)
        hankel = jnp.tile(vals, (1, n + 1))[:, :n * 2 * n].reshape(-1, n, 2 * n)[:, :, :n]
        return hankel[:, ::-1, :]

    return toeplitz(-(n - 1)), toeplitz(1)


def kernel(x, norm_g, final_g, rel_bias, a_w_in, a_w_out, b_w_in, b_ln_g, b_ln_b, b_w_s,
           b_b_s, b_w_out):
    bsz, seq, d = x.shape
    assert norm_g.shape[0] == 2 and a_w_in.shape[0] == 1 and b_w_in.shape[0] == 1
    m = bsz * seq
    n_qi = IDX_HEADS * IDX_DIM
    topk = min(TOPK_MAX, seq // 4)

    cols = a_w_in.shape[2]
    w_pad = jnp.pad(a_w_in[0], ((0, 0), (0, 4 * A_WIDTH + n_qi + LANES - cols))).astype(_BF16)
    qk, v_t, z, qi_t, ki, wi_t = _proj_a(x, norm_g[0:1], w_pad, tm=512)
    words = _indexer(ki, qi_t, wi_t, tq=KEY_TILE, topk=topk)
    t0, t1 = _bias_tables(rel_bias)
    x1 = _attention(qk, v_t, words, t0, t1, z, x, a_w_out[0].astype(_BF16), tq=KEY_TILE)

    out = _sgu_layer(x1.reshape(m, d), norm_g[1:2], b_w_in[0].astype(_BF16),
                     b_ln_g[0:1], b_ln_b[0:1], b_w_s[0], jnp.transpose(b_b_s[0]),
                     b_w_out[0].astype(_BF16), final_g.reshape(1, d), tm=256)
    return out.reshape(bsz, seq, d)
```

```python
import functools
import math

import numpy as np
import jax
import jax.numpy as jnp
from jax import lax
from jax.experimental import pallas as pl
from jax.experimental.pallas import tpu as pltpu

CHUNK = 64
RMS_EPS = 1e-6
LN_EPS = 1e-5

A_HEADS = 8
A_HEAD_DIM = 128
A_WIDTH = A_HEADS * A_HEAD_DIM
IDX_HEADS = 8
IDX_DIM = 64
TOPK_MAX = 256

REL_BUCKETS = 32
REL_MAX_DIST = 128

SGU_CHUNK = 128
B_GROUPS = 8

LANES = 128
KEY_TILE = 512
WORD_BITS = 32
RADIX_GROUP = 4
STAGE_PAD = 16
VMEM_LIMIT = 56 * 1024 * 1024

LOG2_E = math.log2(math.e)
Q_SCALE = A_HEAD_DIM ** -0.5 * LOG2_E

INT_MIN = -2 ** 31
INT_MAX = 2 ** 31 - 1
NEG_BIG = -1e30

_BF16 = jnp.bfloat16
_F32 = jnp.float32
_I32 = jnp.int32


def _const_spec(shape):
    zeros = (0,) * len(shape)
    return pl.BlockSpec(shape, lambda *_: zeros, pipeline_mode=pl.Buffered(1))


def _rms_normalize(x, g):
    return x * lax.rsqrt(jnp.mean(x * x, axis=-1, keepdims=True) + RMS_EPS) * g


def _silu(z):
    return z / (1.0 + jnp.exp(-z))


def _proj_a_kernel(x_ref, g_ref, w_ref, qk_ref, vt_ref, z_ref, qit_ref, ki_ref, wit_ref):
    hb = _rms_normalize(x_ref[...], g_ref[...]).astype(_BF16)
    tm = hb.shape[0]

    def mm(lo, width):
        return jnp.dot(hb, w_ref[:, lo:lo + width], preferred_element_type=_F32)

    step = 512
    for c in range(0, 2 * A_WIDTH, step):
        scale = Q_SCALE if c < A_WIDTH else 1.0
        qk_ref[:, c:c + step] = (mm(c, step) * scale).astype(_BF16)
    pad_rows = lax.broadcasted_iota(_I32, (STAGE_PAD, tm), 0)
    for c in range(0, A_WIDTH, step):
        v_c = mm(2 * A_WIDTH + c, step)
        for hh in range(step // A_HEAD_DIM):
            h = c // A_HEAD_DIM + hh
            v_h = v_c[:, hh * A_HEAD_DIM:(hh + 1) * A_HEAD_DIM]
            vt_ref[h, 0:A_HEAD_DIM, :] = jnp.transpose(v_h).astype(_BF16)
            vt_ref[h, A_HEAD_DIM:, :] = jnp.where(pad_rows == 0, 1.0, 0.0).astype(_BF16)
    for c in range(0, A_WIDTH, step):
        z_ref[:, c:c + step] = mm(3 * A_WIDTH + c, step)
    qi_t = jnp.transpose(mm(4 * A_WIDTH, IDX_HEADS * IDX_DIM))
    for h in range(IDX_HEADS):
        qit_ref[h] = qi_t[h * IDX_DIM:(h + 1) * IDX_DIM, :].astype(_BF16)
    tail = mm(4 * A_WIDTH + IDX_HEADS * IDX_DIM, LANES)
    ki_ref[...] = tail[:, 0:IDX_DIM].astype(_BF16)
    wit_ref[...] = jnp.transpose(tail)[IDX_DIM:IDX_DIM + IDX_HEADS, :]


def _proj_a(x, g, w_pad, tm):
    bsz, seq, d = x.shape
    row = lambda b, i: (b, i, 0)
    return pl.pallas_call(
        _proj_a_kernel,
        grid=(bsz, seq // tm),
        in_specs=[
            pl.BlockSpec((None, tm, d), row),
            _const_spec((1, d)),
            _const_spec(w_pad.shape),
        ],
        out_specs=[
            pl.BlockSpec((None, tm, 2 * A_WIDTH), row),
            pl.BlockSpec((None, A_HEADS, A_HEAD_DIM + STAGE_PAD, tm), lambda b, i: (b, 0, 0, i)),
            pl.BlockSpec((None, tm, A_WIDTH), row),
            pl.BlockSpec((None, IDX_HEADS, IDX_DIM, tm), lambda b, i: (b, 0, 0, i)),
            pl.BlockSpec((None, tm, IDX_DIM), row),
            pl.BlockSpec((None, IDX_HEADS, tm), lambda b, i: (b, 0, i)),
        ],
        out_shape=[
            jax.ShapeDtypeStruct((bsz, seq, 2 * A_WIDTH), _BF16),
            jax.ShapeDtypeStruct((bsz, A_HEADS, A_HEAD_DIM + STAGE_PAD, seq), _BF16),
            jax.ShapeDtypeStruct((bsz, seq, A_WIDTH), _F32),
            jax.ShapeDtypeStruct((bsz, IDX_HEADS, IDX_DIM, seq), _BF16),
            jax.ShapeDtypeStruct((bsz, seq, IDX_DIM), _BF16),
            jax.ShapeDtypeStruct((bsz, IDX_HEADS, seq), _F32),
        ],
        compiler_params=pltpu.CompilerParams(
            dimension_semantics=("arbitrary", "arbitrary"), vmem_limit_bytes=VMEM_LIMIT),
        name="dsa_in_proj",
    )(x, g, w_pad)


def _bit_transpose(vregs):
    a = list(reversed(vregs))
    j, mask = 16, 0x0000FFFF
    while j:
        k = 0
        while k < WORD_BITS:
            t = (a[k] ^ lax.shift_right_logical(a[k + j], jnp.int32(j))) & mask
            a[k] = a[k] ^ t
            a[k + j] = a[k + j] ^ (t << j)
            k = (k + j + 1) & ~j
        j >>= 1
        mask = (mask ^ (mask << j)) & 0xFFFFFFFF
        if mask >= 2 ** 31:
            mask -= 2 ** 32
    return list(reversed(a))


def _indexer_kernel(ki_ref, qit_ref, wit_ref, out_ref, p_sc, c_sc, g_sc, u0_sc, u1_sc,
                    *, tq, topk):
    i = pl.program_id(1)
    nlg = tq // LANES
    half = KEY_TILE // 2
    nj = i + 1
    wscaled = wit_ref[...] * (IDX_HEADS ** -0.5 * IDX_DIM ** -0.5)

    rows = 64
    u_bufs = (u0_sc, u1_sc)
    off = pl.multiple_of(jnp.minimum(i, 0) * 8, 8)

    def planes_of_tile(j, diagonal):
        base = pl.multiple_of(j * KEY_TILE, KEY_TILE)
        for hf in range(2):
            u_buf = u_bufs[hf]
            for c in range(half // rows):
                r0 = hf * half + c * rows
                kt = ki_ref[pl.ds(base + r0, rows), :]
                acc = jnp.zeros((rows, tq), _F32)
                for h in range(IDX_HEADS):
                    d = jnp.dot(kt, qit_ref[h], preferred_element_type=_F32)
                    acc = acc + jnp.maximum(d, 0.0) * wscaled[h:h + 1, :]
                bits = lax.bitcast_convert_type(acc, _I32)
                u = bits ^ ((bits >> 31) | INT_MIN)
                if diagonal:
                    s_chunk = (r0 + lax.broadcasted_iota(_I32, (rows, tq), 0)) // CHUNK
                    t_chunk = lax.broadcasted_iota(_I32, (rows, tq), 1) // CHUNK
                    u = jnp.where(s_chunk <= t_chunk, u, 0)
                u_buf[pl.ds(off + c * rows, rows), :] = u
            for lg in range(nlg):
                blk = [u_buf[pl.ds(off + 8 * m, 8), lg * LANES:(lg + 1) * LANES]
                       for m in range(WORD_BITS)]
                planes = _bit_transpose(blk)
                for b in range(WORD_BITS):
                    p_sc[j, hf, lg, 8 * b:8 * b + 8, :] = planes[b]

    def fill(j, carry):
        planes_of_tile(j, diagonal=False)
        return carry

    @pl.when((pl.program_id(0) == 0) & (i == 0))
    def _():
        def zero_tile(j, carry):
            p_sc[j] = jnp.zeros(p_sc.shape[1:], _I32)
            return carry
        lax.fori_loop(0, p_sc.shape[0], zero_tile, 0)

    lax.fori_loop(0, i, fill, 0)
    planes_of_tile(i, diagonal=True)

    ntiles = p_sc.shape[0]
    ngroups = lax.div(nj + RADIX_GROUP - 1, RADIX_GROUP)

    def init(j, carry):
        c_sc[j] = jnp.full(c_sc.shape[1:], jnp.where(j < nj, -1, 0), _I32)
        g_sc[j] = jnp.zeros(g_sc.shape[1:], _I32)
        return carry

    lax.fori_loop(0, jnp.minimum(ngroups * RADIX_GROUP, ntiles), init, 0)

    def per_query_sum(parts):
        return [jnp.broadcast_to(jnp.sum(p, axis=0, keepdims=True), (8, LANES)) for p in parts]

    def count(word_fn):
        def body(j, acc):
            acc = list(acc)
            for hf in range(2):
                for lg in range(nlg):
                    acc[lg] = acc[lg] + lax.population_count(word_fn(j, hf, lg))
            return tuple(acc)
        zero = jnp.zeros((8, LANES), _I32)
        return per_query_sum(lax.fori_loop(0, nj, body, (zero,) * nlg))

    def radix_step(step, carry):
        need, found, ones = carry
        b = WORD_BITS - 1 - step
        b8 = pl.multiple_of(b * 8, 8)
        next8 = pl.multiple_of(jnp.maximum(b - 1, 0) * 8, 8)
        take = [ones[lg] >= need[lg] for lg in range(nlg)]

        def sweep(grp, acc):
            acc = list(acc)
            for jj in range(RADIX_GROUP):
                j = grp * RADIX_GROUP + jj
                for hf in range(2):
                    for lg in range(nlg):
                        cand = c_sc[j, hf, lg]
                        hit = cand & p_sc[j, hf, lg, pl.ds(b8, 8), :]
                        keep = jnp.where(take[lg], hit, cand ^ hit)
                        c_sc[j, hf, lg] = keep
                        g_sc[j, hf, lg] = g_sc[j, hf, lg] | jnp.where(take[lg], 0, hit)
                        acc[lg] = acc[lg] + lax.population_count(
                            keep & p_sc[j, hf, lg, pl.ds(next8, 8), :])
            return tuple(acc)

        zero = jnp.zeros((8, LANES), _I32)
        nxt = per_query_sum(lax.fori_loop(0, ngroups, sweep, (zero,) * nlg))
        need = tuple(jnp.where(take[lg], need[lg], need[lg] - ones[lg]) for lg in range(nlg))
        found = tuple(jnp.where(take[lg], 1, found[lg]) for lg in range(nlg))
        return need, found, tuple(nxt)

    need0 = (jnp.full((8, LANES), topk, _I32),) * nlg
    found0 = (jnp.zeros((8, LANES), _I32),) * nlg
    top8 = (WORD_BITS - 1) * 8
    ones0 = tuple(count(lambda j, hf, lg: p_sc[j, hf, lg, top8:top8 + 8, :]))
    need, found, _ = lax.fori_loop(0, WORD_BITS, radix_step, (need0, found0, ones0))
    found = [f != 0 for f in found]

    n_ties = count(lambda j, hf, lg: c_sc[j, hf, lg])
    straddle = [found[lg] & (n_ties[lg] > need[lg]) for lg in range(nlg)]
    any_straddle = functools.reduce(
        jnp.maximum, [jnp.max(jnp.where(s, 1.0, 0.0)) for s in straddle]) > 0.0
    sub = lax.broadcasted_iota(_I32, (8, LANES), 0)

    def upto(limit, j, hf):
        n = jnp.clip(((limit - (j * KEY_TILE + hf * half + sub)) >> 3) + 1, 0, WORD_BITS)
        return jnp.where(n >= WORD_BITS, -1, (jnp.int32(1) << jnp.minimum(n, WORD_BITS - 1)) - 1)

    def tie_limit():
        nbits = max(1, int(math.ceil(math.log2(p_sc.shape[0] * KEY_TILE))))

        def step(t, limit):
            cand = [limit[lg] | (jnp.int32(1) << (jnp.int32(nbits - 1) - t)) for lg in range(nlg)]
            before = count(lambda j, hf, lg: c_sc[j, hf, lg] & upto(cand[lg] - 1, j, hf))
            return tuple(jnp.where(before[lg] < need[lg], cand[lg], limit[lg])
                         for lg in range(nlg))

        limit = lax.fori_loop(0, nbits, step, (jnp.zeros((8, LANES), _I32),) * nlg)
        return tuple(jnp.where(straddle[lg], limit[lg], INT_MAX) for lg in range(nlg))

    limit = lax.cond(any_straddle, tie_limit,
                     lambda: (jnp.full((8, LANES), INT_MAX, _I32),) * nlg)

    def emit(j, carry):
        for hf in range(2):
            for lg in range(nlg):
                ties = jnp.where(found[lg], c_sc[j, hf, lg] & upto(limit[lg], j, hf), 0)
                out_ref[lg, j, hf] = g_sc[j, hf, lg] | ties
        return carry

    lax.fori_loop(0, nj, emit, 0)

    def clear(j, carry):
        for lg in range(nlg):
            out_ref[lg, j] = jnp.zeros((2, 8, LANES), _I32)
        return carry

    lax.fori_loop(nj, p_sc.shape[0], clear, 0)


def _indexer(ki, qi_t, wi_t, tq, topk):
    bsz, seq, _ = ki.shape
    nt = seq // KEY_TILE
    nlg = tq // LANES
    assert seq % KEY_TILE == 0 and tq == KEY_TILE and KEY_TILE % (2 * 8 * WORD_BITS) == 0
    assert nt % RADIX_GROUP == 0
    return pl.pallas_call(
        functools.partial(_indexer_kernel, tq=tq, topk=topk),
        grid=(bsz, seq // tq),
        in_specs=[
            pl.BlockSpec((None, seq, IDX_DIM), lambda b, i: (b, 0, 0),
                         pipeline_mode=pl.Buffered(1)),
            pl.BlockSpec((None, IDX_HEADS, IDX_DIM, tq), lambda b, i: (b, 0, 0, i)),
            pl.BlockSpec((None, IDX_HEADS, tq), lambda b, i: (b, 0, i)),
        ],
        out_specs=pl.BlockSpec((None, nlg, nt, 2, 8, LANES), lambda b, i: (b, i, 0, 0, 0, 0)),
        out_shape=jax.ShapeDtypeStruct((bsz, seq // LANES, nt, 2, 8, LANES), _I32),
        scratch_shapes=[
            pltpu.VMEM((nt, 2, nlg, KEY_TILE // 2, LANES), _I32),
            pltpu.VMEM((nt, 2, nlg, 8, LANES), _I32),
            pltpu.VMEM((nt, 2, nlg, 8, LANES), _I32),
            pltpu.VMEM((KEY_TILE // 2 + 8, tq), _I32),
            pltpu.VMEM((KEY_TILE // 2 + 8, tq), _I32),
        ],
        compiler_params=pltpu.CompilerParams(
            dimension_semantics=("arbitrary", "arbitrary"), vmem_limit_bytes=VMEM_LIMIT),
        name="dsa_indexer_topk",
    )(ki, qi_t, wi_t)


def _attn_kernel(bt_ref, it_ref, jt_ref, jbt_ref, q_ref, ka_ref, kb_ref, vta_ref, vtb_ref, wa_ref, wb_ref,
                 t0_ref, t1_ref, z_ref, x_ref, wo_ref, o_ref, m_sc, l_sc, acc_sc, mask_sc,
                 s0_sc, s1_sc, s2_sc, p0_sc, p1_sc, *, tq, tk):
    step = pl.program_id(0)
    i = it_ref[step]
    j = jt_ref[step]
    nsub = tq // LANES
    k_refs, vt_refs, words_refs = (ka_ref, kb_ref), (vta_ref, vtb_ref), (wa_ref, wb_ref)

    @pl.when(j == 0)
    def _():
        m_sc[...] = jnp.full(m_sc.shape, NEG_BIG, _F32)
        l_sc[...] = jnp.zeros(l_sc.shape, _F32)
        acc_sc[...] = jnp.zeros(acc_sc.shape, _F32)

    def build_mask(tile):
        for hf in range(2):
            for lg in range(tq // LANES):
                w = words_refs[tile][lg, hf]
                for m in range(WORD_BITS):
                    r0 = hf * (tk // 2) + 8 * m
                    mask_sc[tile, r0:r0 + 8, lg * LANES:(lg + 1) * LANES] = jnp.where(
                        (w << (WORD_BITS - 1 - m)) < 0, 0.0, -jnp.inf)

    rows = 64
    s_bufs = (s0_sc, s1_sc, s2_sc)
    p_bufs = (p0_sc, p1_sc)
    off = pl.multiple_of(jnp.minimum(j, 0) * STAGE_PAD, STAGE_PAD)

    def fold_rows(x, op):
        return op(x.reshape(x.shape[0] // 8, 8, tq), axis=0)

    def bias_block(h, kind, key_blk, qry_blk, r):
        if kind == "diag" and key_blk == qry_blk:
            return t0_ref[h, r:r + rows, :]
        if kind == "diag" and key_blk + 1 == qry_blk:
            return t1_ref[h, r:r + rows, :]
        if kind == "sub" and key_blk == nsub - 1 and qry_blk == 0:
            return t1_ref[h, r:r + rows, :]
        return None

    def qk(unit):
        tile, h = unit
        hs = slice(h * A_HEAD_DIM, (h + 1) * A_HEAD_DIM)
        return lax.dot_general(k_refs[tile][:, hs], q_ref[:, hs], (((1,), (1,)), ((), ())),
                               preferred_element_type=_F32)

    def stage_logits(n, unit, kind, s, r0, m8):
        tile, h = unit
        t = s[r0:r0 + rows, :] + mask_sc[tile, r0:r0 + rows, :]
        if kind != "far":
            parts = []
            for c in range(nsub):
                blk = t[:, c * LANES:(c + 1) * LANES]
                bias = bias_block(h, kind, r0 // LANES, c, r0 % LANES)
                parts.append(blk if bias is None else blk + bias)
            t = jnp.concatenate(parts, axis=1)
        s_bufs[n % len(s_bufs)][pl.ds(off + r0, rows), :] = t
        return fold_rows(t, jnp.max) if m8 is None else jnp.maximum(m8, fold_rows(t, jnp.max))

    def run_tiles(kinds):
        units = [(tile, h) for tile in range(len(kinds)) for h in range(A_HEADS)]
        for tile in range(len(kinds)):
            build_mask(tile)
        logits = {0: qk(units[0])}
        if len(units) > 1:
            logits[1] = qk(units[1])
        m8 = None
        for r0 in range(0, tk, rows):
            m8 = stage_logits(0, units[0], kinds[0], logits[0], r0, m8)
        for n, (tile, h) in enumerate(units):
            hs = slice(h * A_HEAD_DIM, (h + 1) * A_HEAD_DIM)
            s_buf, p_buf = s_bufs[n % len(s_bufs)], p_bufs[n % len(p_bufs)]
            m_prev = m_sc[h:h + 1, :]
            m_new = jnp.maximum(m_prev, jnp.max(m8, axis=0, keepdims=True))
            alpha = jnp.exp2(m_prev - m_new)
            last = n + 1 == len(units)
            if n + 2 < len(units):
                logits[n + 2] = qk(units[n + 2])
            m8 = None
            for r0 in range(0, tk, rows):
                if not last:
                    m8 = stage_logits(n + 1, units[n + 1], kinds[units[n + 1][0]],
                                      logits[n + 1], r0, m8)
                p = jnp.exp2(s_buf[pl.ds(off + r0, rows), :] - m_new)
                p_buf[pl.ds(off + r0, rows), :] = p.astype(_BF16)
            logits.pop(n, None)
            pv = jnp.dot(vt_refs[tile][h], p_buf[pl.ds(off, tk), :], preferred_element_type=_F32)
            acc_sc[hs, :] = alpha * acc_sc[hs, :] + pv[0:A_HEAD_DIM, :]
            l_sc[h:h + 1, :] = alpha * l_sc[h:h + 1, :] + pv[A_HEAD_DIM:A_HEAD_DIM + 1, :]
            m_sc[h:h + 1, :] = m_new

    @pl.when(j + 1 < i - 1)
    def _():
        run_tiles(("far", "far"))

    @pl.when(j + 1 == i - 1)
    def _():
        run_tiles(("far", "sub"))

    @pl.when(j + 1 == i)
    def _():
        run_tiles(("sub", "diag"))

    @pl.when(j == i)
    def _():
        run_tiles(("diag",))

    @pl.when(j + 1 >= i)
    def _():
        parts = []
        for h in range(A_HEADS):
            hs = slice(h * A_HEAD_DIM, (h + 1) * A_HEAD_DIM)
            out_h = jnp.transpose(acc_sc[hs, :] / l_sc[h:h + 1, :])
            parts.append((out_h * _silu(z_ref[:, hs])).astype(_BF16))
        gated = jnp.concatenate(parts, axis=1)
        o_ref[...] = x_ref[...] + jnp.dot(gated, wo_ref[...], preferred_element_type=_F32)


def _attention(qk, v_t, words_t, t0, t1, z, x, wo, tq):
    bsz, seq, d = x.shape
    tk = KEY_TILE
    assert tq == tk and seq % tq == 0
    nq = seq // tq
    bt, it, jt, jbt = [], [], [], []
    for b in range(bsz):
        for i in range(nq):
            for j in range(0, i + 1, 2):
                bt.append(b)
                it.append(i)
                jt.append(j)
                jbt.append(min(j + 1, i))
    tables = [jnp.asarray(np.asarray(t, np.int32)) for t in (bt, it, jt, jbt)]
    vt_block = (None, A_HEADS, A_HEAD_DIM + STAGE_PAD, tk)
    words_block = (None, tq // LANES, None, 2, 8, LANES)

    grid_spec = pltpu.PrefetchScalarGridSpec(
        num_scalar_prefetch=4,
        grid=(len(bt),),
        in_specs=[
            pl.BlockSpec((None, tq, A_WIDTH), lambda s, b, i, j, jb: (b[s], i[s], 0)),
            pl.BlockSpec((None, tk, A_WIDTH), lambda s, b, i, j, jb: (b[s], j[s], 1)),
            pl.BlockSpec((None, tk, A_WIDTH), lambda s, b, i, j, jb: (b[s], jb[s], 1)),
            pl.BlockSpec(vt_block, lambda s, b, i, j, jb: (b[s], 0, 0, j[s])),
            pl.BlockSpec(vt_block, lambda s, b, i, j, jb: (b[s], 0, 0, jb[s])),
            pl.BlockSpec(words_block, lambda s, b, i, j, jb: (b[s], i[s], j[s], 0, 0, 0)),
            pl.BlockSpec(words_block, lambda s, b, i, j, jb: (b[s], i[s], jb[s], 0, 0, 0)),
            _const_spec(t0.shape),
            _const_spec(t1.shape),
            pl.BlockSpec((None, tq, A_WIDTH), lambda s, b, i, j, jb: (b[s], i[s], 0)),
            pl.BlockSpec((None, tq, d), lambda s, b, i, j, jb: (b[s], i[s], 0)),
            _const_spec(wo.shape),
        ],
        out_specs=pl.BlockSpec((None, tq, d), lambda s, b, i, j, jb: (b[s], i[s], 0)),
        scratch_shapes=[
            pltpu.VMEM((A_HEADS, tq), _F32),
            pltpu.VMEM((A_HEADS, tq), _F32),
            pltpu.VMEM((A_WIDTH, tq), _F32),
            pltpu.VMEM((2, tk, tq), _F32),
            pltpu.VMEM((tk + STAGE_PAD, tq), _F32),
            pltpu.VMEM((tk + STAGE_PAD, tq), _F32),
            pltpu.VMEM((tk + STAGE_PAD, tq), _F32),
            pltpu.VMEM((tk + STAGE_PAD, tq), _BF16),
            pltpu.VMEM((tk + STAGE_PAD, tq), _BF16),
        ],
    )
    return pl.pallas_call(
        functools.partial(_attn_kernel, tq=tq, tk=tk),
        grid_spec=grid_spec,
        out_shape=jax.ShapeDtypeStruct((bsz, seq, d), _F32),
        compiler_params=pltpu.CompilerParams(
            dimension_semantics=("arbitrary",), vmem_limit_bytes=VMEM_LIMIT),
        name="dsa_attention",
    )(*tables, qk, qk, qk, v_t, v_t, words_t, words_t, t0, t1, z, x, wo)


def _sgu_kernel(x_ref, g_ref, win_ref, lng_ref, lnb_ref, ws_ref, bst_ref, wout_ref, fg_ref,
                o_ref, *, tm, width):
    x = x_ref[...]
    hb = _rms_normalize(x, g_ref[...]).astype(_BF16)
    gdim = width // B_GROUPS

    def mm(lo):
        return jnp.dot(hb, win_ref[:, lo:lo + width], preferred_element_type=_F32)

    v = mm(width)
    mu = jnp.mean(v, axis=-1, keepdims=True)
    vc = v - mu
    var = jnp.mean(vc * vc, axis=-1, keepdims=True)
    vn = ((vc * lax.rsqrt(var + LN_EPS)) * lng_ref[...] + lnb_ref[...]).astype(_BF16)

    row = lax.broadcasted_iota(_I32, (SGU_CHUNK, SGU_CHUNK), 0)
    col = lax.broadcasted_iota(_I32, (SGU_CHUNK, SGU_CHUNK), 1)
    bst = bst_ref[...]
    cols = []
    for g in range(B_GROUPS):
        ws = jnp.where(row >= col, ws_ref[g], 0.0).astype(_BF16)
        bias = bst[:, g:g + 1]
        chunks = []
        for c in range(tm // SGU_CHUNK):
            blk = vn[c * SGU_CHUNK:(c + 1) * SGU_CHUNK, g * gdim:(g + 1) * gdim]
            chunks.append(jnp.dot(ws, blk, preferred_element_type=_F32) + bias)
        cols.append(jnp.concatenate(chunks, axis=0))
    mixed = jnp.concatenate(cols, axis=1)

    y = (mm(0) * mixed * _silu(mm(2 * width))).astype(_BF16)
    x2 = x + jnp.dot(y, wout_ref[...], preferred_element_type=_F32)
    o_ref[...] = _rms_normalize(x2, fg_ref[...])


def _sgu_layer(x2, g, w_in, ln_g, ln_b, w_s, b_s_t, w_out, final_g, tm):
    m, d = x2.shape
    width = w_out.shape[0]
    return pl.pallas_call(
        functools.partial(_sgu_kernel, tm=tm, width=width),
        grid=(m // tm,),
        in_specs=[
            pl.BlockSpec((tm, d), lambda i: (i, 0)),
            _const_spec((1, d)),
            _const_spec(w_in.shape),
            _const_spec((1, width)),
            _const_spec((1, width)),
            _const_spec(w_s.shape),
            _const_spec(b_s_t.shape),
            _const_spec(w_out.shape),
            _const_spec((1, d)),
        ],
        out_specs=pl.BlockSpec((tm, d), lambda i: (i, 0)),
        out_shape=jax.ShapeDtypeStruct((m, d), _F32),
        compiler_params=pltpu.CompilerParams(
            dimension_semantics=("arbitrary",), vmem_limit_bytes=VMEM_LIMIT),
        name="sgu_layer_final_norm",
    )(x2, g, w_in, ln_g, ln_b, w_s, b_s_t, w_out, final_g)


def _t5_bucket(rel):
    half = REL_BUCKETS // 2
    max_exact = half // 2
    ret = jnp.where(rel < 0, half, 0)
    n = jnp.abs(rel)
    nf = jnp.maximum(n, 1).astype(jnp.float32)
    large = max_exact + (jnp.log(nf / max_exact) / math.log(REL_MAX_DIST / max_exact)
                         * (half - max_exact)).astype(jnp.int32)
    large = jnp.minimum(large, half - 1)
    return ret + jnp.where(n < max_exact, n, large)


def _bias_tables(rel_bias):
    n = LANES
    far = rel_bias[_t5_bucket(jnp.int32(4 * REL_MAX_DIST))]

    def toeplitz(first_rel):
        rel = first_rel + jnp.arange(2 * n - 1, dtype=jnp.int32)
        vals = jnp.transpose((rel_bias[_t5_bucket(rel)] - far) * LOG2_E)
        hankel = jnp.tile(vals, (1, n + 1))[:, :n * 2 * n].reshape(-1, n, 2 * n)[:, :, :n]
        return hankel[:, ::-1, :]

    return toeplitz(-(n - 1)), toeplitz(1)


def kernel(x, norm_g, final_g, rel_bias, a_w_in, a_w_out, b_w_in, b_ln_g, b_ln_b, b_w_s,
           b_b_s, b_w_out):
    bsz, seq, d = x.shape
    assert norm_g.shape[0] == 2 and a_w_in.shape[0] == 1 and b_w_in.shape[0] == 1
    m = bsz * seq
    n_qi = IDX_HEADS * IDX_DIM
    topk = min(TOPK_MAX, seq // 4)

    cols = a_w_in.shape[2]
    w_pad = jnp.pad(a_w_in[0], ((0, 0), (0, 4 * A_WIDTH + n_qi + LANES - cols))).astype(_BF16)
    qk, v_t, z, qi_t, ki, wi_t = _proj_a(x, norm_g[0:1], w_pad, tm=512)
    words = _indexer(ki, qi_t, wi_t, tq=KEY_TILE, topk=topk)
    t0, t1 = _bias_tables(rel_bias)
    x1 = _attention(qk, v_t, words, t0, t1, z, x, a_w_out[0].astype(_BF16), tq=KEY_TILE)

    out = _sgu_layer(x1.reshape(m, d), norm_g[1:2], b_w_in[0].astype(_BF16),
                     b_ln_g[0:1], b_ln_b[0:1], b_w_s[0], jnp.transpose(b_b_s[0]),
                     b_w_out[0].astype(_BF16), final_g.reshape(1, d), tm=256)
    return out.reshape(bsz, seq, d)
```

```python
import functools
import math

import numpy as np
import jax
import jax.numpy as jnp
from jax import lax
from jax.experimental import pallas as pl
from jax.experimental.pallas import tpu as pltpu

CHUNK = 64
RMS_EPS = 1e-6
LN_EPS = 1e-5

A_HEADS = 8
A_HEAD_DIM = 128
A_WIDTH = A_HEADS * A_HEAD_DIM
IDX_HEADS = 8
IDX_DIM = 64
TOPK_MAX = 256

REL_BUCKETS = 32
REL_MAX_DIST = 128

SGU_CHUNK = 128
B_GROUPS = 8

LANES = 128
KEY_TILE = 512
WORD_BITS = 32
RADIX_GROUP = 4
STAGE_PAD = 16
VMEM_LIMIT = 56 * 1024 * 1024

LOG2_E = math.log2(math.e)
Q_SCALE = A_HEAD_DIM ** -0.5 * LOG2_E

INT_MIN = -2 ** 31
INT_MAX = 2 ** 31 - 1
NEG_BIG = -1e30

_BF16 = jnp.bfloat16
_F32 = jnp.float32
_I32 = jnp.int32


def _const_spec(shape):
    zeros = (0,) * len(shape)
    return pl.BlockSpec(shape, lambda *_: zeros, pipeline_mode=pl.Buffered(1))


def _rms_normalize(x, g):
    return x * lax.rsqrt(jnp.mean(x * x, axis=-1, keepdims=True) + RMS_EPS) * g


def _silu(z):
    return z / (1.0 + jnp.exp(-z))


def _proj_a_kernel(x_ref, g_ref, w_ref, qk_ref, qt_ref, vt_ref, z_ref, qit_ref, ki_ref, wit_ref):
    hb = _rms_normalize(x_ref[...], g_ref[...]).astype(_BF16)
    tm = hb.shape[0]

    def mm(lo, width):
        return jnp.dot(hb, w_ref[:, lo:lo + width], preferred_element_type=_F32)

    step = 512
    for c in range(0, 2 * A_WIDTH, step):
        scale = Q_SCALE if c < A_WIDTH else 1.0
        qk_c = mm(c, step) * scale
        qk_ref[:, c:c + step] = qk_c.astype(_BF16)
        if c < A_WIDTH:
            for hh in range(step // A_HEAD_DIM):
                q_h = qk_c[:, hh * A_HEAD_DIM:(hh + 1) * A_HEAD_DIM]
                qt_ref[c // A_HEAD_DIM + hh] = jnp.transpose(q_h).astype(_BF16)
    pad_rows = lax.broadcasted_iota(_I32, (STAGE_PAD, tm), 0)
    for c in range(0, A_WIDTH, step):
        v_c = mm(2 * A_WIDTH + c, step)
        for hh in range(step // A_HEAD_DIM):
            h = c // A_HEAD_DIM + hh
            v_h = v_c[:, hh * A_HEAD_DIM:(hh + 1) * A_HEAD_DIM]
            vt_ref[h, 0:A_HEAD_DIM, :] = jnp.transpose(v_h).astype(_BF16)
            vt_ref[h, A_HEAD_DIM:, :] = jnp.where(pad_rows == 0, 1.0, 0.0).astype(_BF16)
    for c in range(0, A_WIDTH, step):
        z_ref[:, c:c + step] = mm(3 * A_WIDTH + c, step)
    qi_t = jnp.transpose(mm(4 * A_WIDTH, IDX_HEADS * IDX_DIM))
    for h in range(IDX_HEADS):
        qit_ref[h] = qi_t[h * IDX_DIM:(h + 1) * IDX_DIM, :].astype(_BF16)
    tail = mm(4 * A_WIDTH + IDX_HEADS * IDX_DIM, LANES)
    ki_ref[...] = tail[:, 0:IDX_DIM].astype(_BF16)
    wit_ref[...] = jnp.transpose(tail)[IDX_DIM:IDX_DIM + IDX_HEADS, :]


def _proj_a(x, g, w_pad, tm):
    bsz, seq, d = x.shape
    row = lambda b, i: (b, i, 0)
    return pl.pallas_call(
        _proj_a_kernel,
        grid=(bsz, seq // tm),
        in_specs=[
            pl.BlockSpec((None, tm, d), row),
            _const_spec((1, d)),
            _const_spec(w_pad.shape),
        ],
        out_specs=[
            pl.BlockSpec((None, tm, 2 * A_WIDTH), row),
            pl.BlockSpec((None, A_HEADS, A_HEAD_DIM, tm), lambda b, i: (b, 0, 0, i)),
            pl.BlockSpec((None, A_HEADS, A_HEAD_DIM + STAGE_PAD, tm), lambda b, i: (b, 0, 0, i)),
            pl.BlockSpec((None, tm, A_WIDTH), row),
            pl.BlockSpec((None, IDX_HEADS, IDX_DIM, tm), lambda b, i: (b, 0, 0, i)),
            pl.BlockSpec((None, tm, IDX_DIM), row),
            pl.BlockSpec((None, IDX_HEADS, tm), lambda b, i: (b, 0, i)),
        ],
        out_shape=[
            jax.ShapeDtypeStruct((bsz, seq, 2 * A_WIDTH), _BF16),
            jax.ShapeDtypeStruct((bsz, A_HEADS, A_HEAD_DIM, seq), _BF16),
            jax.ShapeDtypeStruct((bsz, A_HEADS, A_HEAD_DIM + STAGE_PAD, seq), _BF16),
            jax.ShapeDtypeStruct((bsz, seq, A_WIDTH), _F32),
            jax.ShapeDtypeStruct((bsz, IDX_HEADS, IDX_DIM, seq), _BF16),
            jax.ShapeDtypeStruct((bsz, seq, IDX_DIM), _BF16),
            jax.ShapeDtypeStruct((bsz, IDX_HEADS, seq), _F32),
        ],
        compiler_params=pltpu.CompilerParams(
            dimension_semantics=("arbitrary", "arbitrary"), vmem_limit_bytes=VMEM_LIMIT),
        name="dsa_in_proj",
    )(x, g, w_pad)


def _bit_transpose(vregs):
    a = list(reversed(vregs))
    j, mask = 16, 0x0000FFFF
    while j:
        k = 0
        while k < WORD_BITS:
            t = (a[k] ^ lax.shift_right_logical(a[k + j], jnp.int32(j))) & mask
            a[k] = a[k] ^ t
            a[k + j] = a[k + j] ^ (t << j)
            k = (k + j + 1) & ~j
        j >>= 1
        mask = (mask ^ (mask << j)) & 0xFFFFFFFF
        if mask >= 2 ** 31:
            mask -= 2 ** 32
    return list(reversed(a))


def _indexer_kernel(ki_ref, qit_ref, wit_ref, out_ref, p_sc, c_sc, g_sc, adm_sc,
                    ua0_sc, ua1_sc, ub0_sc, ub1_sc, *, tq, topk):
    i = pl.program_id(1)
    nlg = tq // LANES
    half = KEY_TILE // 2
    nj = i + 1
    wscaled = wit_ref[...] * (IDX_HEADS ** -0.5 * IDX_DIM ** -0.5)

    rows = 64
    off = pl.multiple_of(jnp.minimum(i, 0) * 8, 8)

    @pl.when((pl.program_id(0) == 0) & (i == 0))
    def _():
        def zero_tile(j, carry):
            p_sc[j] = jnp.zeros(p_sc.shape[1:], _I32)
            return carry
        lax.fori_loop(0, p_sc.shape[0], zero_tile, 0)
        sub8 = lax.broadcasted_iota(_I32, (8, LANES), 0)
        lane8 = lax.broadcasted_iota(_I32, (8, LANES), 1)
        for hf in range(2):
            for lg in range(nlg):
                t_chunk = (lg * LANES + lane8) // CHUNK
                word = jnp.zeros((8, LANES), _I32)
                for m in range(WORD_BITS):
                    s_chunk = (hf * half + 8 * m + sub8) // CHUNK
                    bit = INT_MIN if m == WORD_BITS - 1 else 1 << m
                    word = word | jnp.where(s_chunk <= t_chunk, bit, 0)
                adm_sc[0, hf, lg] = jnp.full((8, LANES), -1, _I32)
                adm_sc[1, hf, lg] = word

    def score_keys(j, u_bufs):
        base = pl.multiple_of(j * KEY_TILE, KEY_TILE)
        for hf in range(2):
            for c in range(half // rows):
                r0 = hf * half + c * rows
                kt = ki_ref[pl.ds(base + r0, rows), :]
                acc = jnp.zeros((rows, tq), _F32)
                for h in range(IDX_HEADS):
                    d = jnp.dot(kt, qit_ref[h], preferred_element_type=_F32)
                    acc = acc + jnp.maximum(d, 0.0) * wscaled[h:h + 1, :]
                u_bufs[hf][pl.ds(off + c * rows, rows), :] = lax.bitcast_convert_type(acc, _I32)

    def planes_of(j, u_bufs):
        diag = jnp.where(j == i, 1, 0)
        for hf in range(2):
            for lg in range(nlg):
                blk = [u_bufs[hf][pl.ds(off + 8 * m, 8), lg * LANES:(lg + 1) * LANES]
                       for m in range(WORD_BITS)]
                planes = _bit_transpose(blk)
                sign = planes[WORD_BITS - 1]
                adm = adm_sc[diag, hf, lg]
                for b in range(WORD_BITS - 1):
                    p_sc[j, hf, lg, 8 * b:8 * b + 8, :] = (planes[b] ^ sign) & adm
                top = 8 * (WORD_BITS - 1)
                p_sc[j, hf, lg, top:top + 8, :] = ~sign & adm

    bufs_a, bufs_b = (ua0_sc, ua1_sc), (ub0_sc, ub1_sc)
    score_keys(0, bufs_a)
    npairs = lax.div(nj - 1, 2)

    def tile_pair(k, carry):
        score_keys(2 * k + 1, bufs_b)
        planes_of(2 * k, bufs_a)
        score_keys(2 * k + 2, bufs_a)
        planes_of(2 * k + 1, bufs_b)
        return carry

    lax.fori_loop(0, npairs, tile_pair, 0)

    @pl.when(nj - 1 > 2 * npairs)
    def _():
        score_keys(nj - 1, bufs_b)
        planes_of(nj - 2, bufs_a)
        planes_of(nj - 1, bufs_b)

    @pl.when(nj - 1 == 2 * npairs)
    def _():
        planes_of(nj - 1, bufs_a)

    ntiles = p_sc.shape[0]
    ngroups = lax.div(nj + RADIX_GROUP - 1, RADIX_GROUP)

    def init(j, carry):
        c_sc[j] = jnp.full(c_sc.shape[1:], jnp.where(j < nj, -1, 0), _I32)
        g_sc[j] = jnp.zeros(g_sc.shape[1:], _I32)
        return carry

    lax.fori_loop(0, jnp.minimum(ngroups * RADIX_GROUP, ntiles), init, 0)

    def per_query_sum(parts):
        return [jnp.broadcast_to(jnp.sum(p, axis=0, keepdims=True), (8, LANES)) for p in parts]

    def count(word_fn):
        def body(j, acc):
            acc = list(acc)
            for hf in range(2):
                for lg in range(nlg):
                    acc[lg] = acc[lg] + lax.population_count(word_fn(j, hf, lg))
            return tuple(acc)
        zero = jnp.zeros((8, LANES), _I32)
        return per_query_sum(lax.fori_loop(0, nj, body, (zero,) * nlg))

    def radix_step(step, carry):
        need, found, ones = carry
        b = WORD_BITS - 1 - step
        b8 = pl.multiple_of(b * 8, 8)
        next8 = pl.multiple_of(jnp.maximum(b - 1, 0) * 8, 8)
        take = [ones[lg] >= need[lg] for lg in range(nlg)]

        def sweep(grp, acc):
            acc = list(acc)
            for jj in range(RADIX_GROUP):
                j = grp * RADIX_GROUP + jj
                for hf in range(2):
                    for lg in range(nlg):
                        cand = c_sc[j, hf, lg]
                        hit = cand & p_sc[j, hf, lg, pl.ds(b8, 8), :]
                        keep = jnp.where(take[lg], hit, cand ^ hit)
                        c_sc[j, hf, lg] = keep
                        g_sc[j, hf, lg] = g_sc[j, hf, lg] | jnp.where(take[lg], 0, hit)
                        acc[lg] = acc[lg] + lax.population_count(
                            keep & p_sc[j, hf, lg, pl.ds(next8, 8), :])
            return tuple(acc)

        zero = jnp.zeros((8, LANES), _I32)
        nxt = per_query_sum(lax.fori_loop(0, ngroups, sweep, (zero,) * nlg))
        need = tuple(jnp.where(take[lg], need[lg], need[lg] - ones[lg]) for lg in range(nlg))
        found = tuple(jnp.where(take[lg], 1, found[lg]) for lg in range(nlg))
        return need, found, tuple(nxt)

    need0 = (jnp.full((8, LANES), topk, _I32),) * nlg
    found0 = (jnp.zeros((8, LANES), _I32),) * nlg
    top8 = (WORD_BITS - 1) * 8
    ones0 = tuple(count(lambda j, hf, lg: p_sc[j, hf, lg, top8:top8 + 8, :]))
    need, found, _ = lax.fori_loop(0, WORD_BITS, radix_step, (need0, found0, ones0))
    found = [f != 0 for f in found]

    n_ties = count(lambda j, hf, lg: c_sc[j, hf, lg])
    straddle = [found[lg] & (n_ties[lg] > need[lg]) for lg in range(nlg)]
    any_straddle = functools.reduce(
        jnp.maximum, [jnp.max(jnp.where(s, 1.0, 0.0)) for s in straddle]) > 0.0
    sub = lax.broadcasted_iota(_I32, (8, LANES), 0)

    def upto(limit, j, hf):
        n = jnp.clip(((limit - (j * KEY_TILE + hf * half + sub)) >> 3) + 1, 0, WORD_BITS)
        return jnp.where(n >= WORD_BITS, -1, (jnp.int32(1) << jnp.minimum(n, WORD_BITS - 1)) - 1)

    def tie_limit():
        nbits = max(1, int(math.ceil(math.log2(p_sc.shape[0] * KEY_TILE))))

        def step(t, limit):
            cand = [limit[lg] | (jnp.int32(1) << (jnp.int32(nbits - 1) - t)) for lg in range(nlg)]
            before = count(lambda j, hf, lg: c_sc[j, hf, lg] & upto(cand[lg] - 1, j, hf))
            return tuple(jnp.where(before[lg] < need[lg], cand[lg], limit[lg])
                         for lg in range(nlg))

        limit = lax.fori_loop(0, nbits, step, (jnp.zeros((8, LANES), _I32),) * nlg)
        return tuple(jnp.where(straddle[lg], limit[lg], INT_MAX) for lg in range(nlg))

    limit = lax.cond(any_straddle, tie_limit,
                     lambda: (jnp.full((8, LANES), INT_MAX, _I32),) * nlg)

    def emit(j, carry):
        for hf in range(2):
            for lg in range(nlg):
                ties = jnp.where(found[lg], c_sc[j, hf, lg] & upto(limit[lg], j, hf), 0)
                out_ref[lg, j, hf] = g_sc[j, hf, lg] | ties
        return carry

    lax.fori_loop(0, nj, emit, 0)

    def clear(j, carry):
        for lg in range(nlg):
            out_ref[lg, j] = jnp.zeros((2, 8, LANES), _I32)
        return carry

    lax.fori_loop(nj, p_sc.shape[0], clear, 0)


def _indexer(ki, qi_t, wi_t, tq, topk):
    bsz, seq, _ = ki.shape
    nt = seq // KEY_TILE
    nlg = tq // LANES
    assert seq % KEY_TILE == 0 and tq == KEY_TILE and KEY_TILE % (2 * 8 * WORD_BITS) == 0
    assert nt % RADIX_GROUP == 0
    return pl.pallas_call(
        functools.partial(_indexer_kernel, tq=tq, topk=topk),
        grid=(bsz, seq // tq),
        in_specs=[
            pl.BlockSpec((None, seq, IDX_DIM), lambda b, i: (b, 0, 0),
                         pipeline_mode=pl.Buffered(1)),
            pl.BlockSpec((None, IDX_HEADS, IDX_DIM, tq), lambda b, i: (b, 0, 0, i)),
            pl.BlockSpec((None, IDX_HEADS, tq), lambda b, i: (b, 0, i)),
        ],
        out_specs=pl.BlockSpec((None, nlg, nt, 2, 8, LANES), lambda b, i: (b, i, 0, 0, 0, 0)),
        out_shape=jax.ShapeDtypeStruct((bsz, seq // LANES, nt, 2, 8, LANES), _I32),
        scratch_shapes=[
            pltpu.VMEM((nt, 2, nlg, KEY_TILE // 2, LANES), _I32),
            pltpu.VMEM((nt, 2, nlg, 8, LANES), _I32),
            pltpu.VMEM((nt, 2, nlg, 8, LANES), _I32),
            pltpu.VMEM((2, 2, nlg, 8, LANES), _I32),
        ] + [pltpu.VMEM((KEY_TILE // 2 + 8, tq), _I32)] * 4,
        compiler_params=pltpu.CompilerParams(
            dimension_semantics=("arbitrary", "arbitrary"), vmem_limit_bytes=VMEM_LIMIT),
        name="dsa_indexer_topk",
    )(ki, qi_t, wi_t)


def _attn_kernel(bt_ref, it_ref, jt_ref, jbt_ref, q_ref, ka_ref, kb_ref, vta_ref, vtb_ref, wa_ref, wb_ref,
                 t0_ref, t1_ref, z_ref, x_ref, wo_ref, o_ref, m_sc, l_sc, acc_sc, mask_sc,
                 s0_sc, s1_sc, s2_sc, p0_sc, p1_sc, *, tq, tk):
    step = pl.program_id(0)
    i = it_ref[step]
    j = jt_ref[step]
    nsub = tq // LANES
    k_refs, vt_refs, words_refs = (ka_ref, kb_ref), (vta_ref, vtb_ref), (wa_ref, wb_ref)

    @pl.when(j == 0)
    def _():
        m_sc[...] = jnp.full(m_sc.shape, NEG_BIG, _F32)
        l_sc[...] = jnp.zeros(l_sc.shape, _F32)
        acc_sc[...] = jnp.zeros(acc_sc.shape, _F32)

    def build_mask(tile):
        for hf in range(2):
            for lg in range(tq // LANES):
                w = words_refs[tile][lg, hf]
                for m in range(WORD_BITS):
                    r0 = hf * (tk // 2) + 8 * m
                    mask_sc[tile, r0:r0 + 8, lg * LANES:(lg + 1) * LANES] = jnp.where(
                        (w << (WORD_BITS - 1 - m)) < 0, 0.0, -jnp.inf)

    rows = 64
    s_bufs = (s0_sc, s1_sc, s2_sc)
    p_bufs = (p0_sc, p1_sc)
    off = pl.multiple_of(jnp.minimum(j, 0) * STAGE_PAD, STAGE_PAD)

    def fold_rows(x, op):
        return op(x.reshape(x.shape[0] // 8, 8, tq), axis=0)

    def bias_block(h, kind, key_blk, qry_blk, r):
        if kind == "diag" and key_blk == qry_blk:
            return t0_ref[h, r:r + rows, :]
        if kind == "diag" and key_blk + 1 == qry_blk:
            return t1_ref[h, r:r + rows, :]
        if kind == "sub" and key_blk == nsub - 1 and qry_blk == 0:
            return t1_ref[h, r:r + rows, :]
        return None

    def qk(unit):
        tile, h = unit
        hs = slice(h * A_HEAD_DIM, (h + 1) * A_HEAD_DIM)
        return jnp.dot(k_refs[tile][:, hs], q_ref[h], preferred_element_type=_F32)

    def stage_logits(n, unit, kind, s, r0, m8):
        tile, h = unit
        t = s[r0:r0 + rows, :] + mask_sc[tile, r0:r0 + rows, :]
        if kind != "far":
            parts = []
            for c in range(nsub):
                blk = t[:, c * LANES:(c + 1) * LANES]
                bias = bias_block(h, kind, r0 // LANES, c, r0 % LANES)
                parts.append(blk if bias is None else blk + bias)
            t = jnp.concatenate(parts, axis=1)
        s_bufs[n % len(s_bufs)][pl.ds(off + r0, rows), :] = t
        return fold_rows(t, jnp.max) if m8 is None else jnp.maximum(m8, fold_rows(t, jnp.max))

    def run_tiles(kinds):
        units = [(tile, h) for tile in range(len(kinds)) for h in range(A_HEADS)]
        for tile in range(len(kinds)):
            build_mask(tile)
        logits = {0: qk(units[0])}
        if len(units) > 1:
            logits[1] = qk(units[1])
        m8 = None
        for r0 in range(0, tk, rows):
            m8 = stage_logits(0, units[0], kinds[0], logits[0], r0, m8)
        for n, (tile, h) in enumerate(units):
            hs = slice(h * A_HEAD_DIM, (h + 1) * A_HEAD_DIM)
            s_buf, p_buf = s_bufs[n % len(s_bufs)], p_bufs[n % len(p_bufs)]
            m_prev = m_sc[h:h + 1, :]
            m_new = jnp.maximum(m_prev, jnp.max(m8, axis=0, keepdims=True))
            alpha = jnp.exp2(m_prev - m_new)
            last = n + 1 == len(units)
            if n + 2 < len(units):
                logits[n + 2] = qk(units[n + 2])
            m8 = None
            for r0 in range(0, tk, rows):
                if not last:
                    m8 = stage_logits(n + 1, units[n + 1], kinds[units[n + 1][0]],
                                      logits[n + 1], r0, m8)
                p = jnp.exp2(s_buf[r0:r0 + rows, :] - m_new)
                p_buf[pl.ds(off + r0, rows), :] = p.astype(_BF16)
            logits.pop(n, None)
            pv = jnp.dot(vt_refs[tile][h], p_buf[0:tk, :], preferred_element_type=_F32)
            acc_sc[hs, :] = alpha * acc_sc[hs, :] + pv[0:A_HEAD_DIM, :]
            l_sc[h:h + 1, :] = alpha * l_sc[h:h + 1, :] + pv[A_HEAD_DIM:A_HEAD_DIM + 1, :]
            m_sc[h:h + 1, :] = m_new

    @pl.when(j + 1 < i - 1)
    def _():
        run_tiles(("far", "far"))

    @pl.when(j + 1 == i - 1)
    def _():
        run_tiles(("far", "sub"))

    @pl.when(j + 1 == i)
    def _():
        run_tiles(("sub", "diag"))

    @pl.when(j == i)
    def _():
        run_tiles(("diag",))

    @pl.when(j + 1 >= i)
    def _():
        parts = []
        for h in range(A_HEADS):
            hs = slice(h * A_HEAD_DIM, (h + 1) * A_HEAD_DIM)
            out_h = jnp.transpose(acc_sc[hs, :] / l_sc[h:h + 1, :])
            parts.append((out_h * _silu(z_ref[:, hs])).astype(_BF16))
        gated = jnp.concatenate(parts, axis=1)
        o_ref[...] = x_ref[...] + jnp.dot(gated, wo_ref[...], preferred_element_type=_F32)


def _attention(q_t, qk, v_t, words_t, t0, t1, z, x, wo, tq):
    bsz, seq, d = x.shape
    tk = KEY_TILE
    assert tq == tk and seq % tq == 0
    nq = seq // tq
    bt, it, jt, jbt = [], [], [], []
    for b in range(bsz):
        for i in range(nq):
            for j in range(0, i + 1, 2):
                bt.append(b)
                it.append(i)
                jt.append(j)
                jbt.append(min(j + 1, i))
    tables = [jnp.asarray(np.asarray(t, np.int32)) for t in (bt, it, jt, jbt)]
    vt_block = (None, A_HEADS, A_HEAD_DIM + STAGE_PAD, tk)
    words_block = (None, tq // LANES, None, 2, 8, LANES)

    grid_spec = pltpu.PrefetchScalarGridSpec(
        num_scalar_prefetch=4,
        grid=(len(bt),),
        in_specs=[
            pl.BlockSpec((None, A_HEADS, A_HEAD_DIM, tq), lambda s, b, i, j, jb: (b[s], 0, 0, i[s])),
            pl.BlockSpec((None, tk, A_WIDTH), lambda s, b, i, j, jb: (b[s], j[s], 1)),
            pl.BlockSpec((None, tk, A_WIDTH), lambda s, b, i, j, jb: (b[s], jb[s], 1)),
            pl.BlockSpec(vt_block, lambda s, b, i, j, jb: (b[s], 0, 0, j[s])),
            pl.BlockSpec(vt_block, lambda s, b, i, j, jb: (b[s], 0, 0, jb[s])),
            pl.BlockSpec(words_block, lambda s, b, i, j, jb: (b[s], i[s], j[s], 0, 0, 0)),
            pl.BlockSpec(words_block, lambda s, b, i, j, jb: (b[s], i[s], jb[s], 0, 0, 0)),
            _const_spec(t0.shape),
            _const_spec(t1.shape),
            pl.BlockSpec((None, tq, A_WIDTH), lambda s, b, i, j, jb: (b[s], i[s], 0)),
            pl.BlockSpec((None, tq, d), lambda s, b, i, j, jb: (b[s], i[s], 0)),
            _const_spec(wo.shape),
        ],
        out_specs=pl.BlockSpec((None, tq, d), lambda s, b, i, j, jb: (b[s], i[s], 0)),
        scratch_shapes=[
            pltpu.VMEM((A_HEADS, tq), _F32),
            pltpu.VMEM((A_HEADS, tq), _F32),
            pltpu.VMEM((A_WIDTH, tq), _F32),
            pltpu.VMEM((2, tk, tq), _F32),
            pltpu.VMEM((tk + STAGE_PAD, tq), _F32),
            pltpu.VMEM((tk + STAGE_PAD, tq), _F32),
            pltpu.VMEM((tk + STAGE_PAD, tq), _F32),
            pltpu.VMEM((tk + STAGE_PAD, tq), _BF16),
            pltpu.VMEM((tk + STAGE_PAD, tq), _BF16),
        ],
    )
    return pl.pallas_call(
        functools.partial(_attn_kernel, tq=tq, tk=tk),
        grid_spec=grid_spec,
        out_shape=jax.ShapeDtypeStruct((bsz, seq, d), _F32),
        compiler_params=pltpu.CompilerParams(
            dimension_semantics=("arbitrary",), vmem_limit_bytes=VMEM_LIMIT),
        name="dsa_attention",
    )(*tables, q_t, qk, qk, v_t, v_t, words_t, words_t, t0, t1, z, x, wo)


def _sgu_kernel(x_ref, g_ref, win_ref, lng_ref, lnb_ref, ws_ref, bst_ref, wout_ref, fg_ref,
                o_ref, *, tm, width):
    x = x_ref[...]
    hb = _rms_normalize(x, g_ref[...]).astype(_BF16)
    gdim = width // B_GROUPS

    def mm(lo):
        return jnp.dot(hb, win_ref[:, lo:lo + width], preferred_element_type=_F32)

    v = mm(width)
    mu = jnp.mean(v, axis=-1, keepdims=True)
    vc = v - mu
    var = jnp.mean(vc * vc, axis=-1, keepdims=True)
    vn = ((vc * lax.rsqrt(var + LN_EPS)) * lng_ref[...] + lnb_ref[...]).astype(_BF16)

    row = lax.broadcasted_iota(_I32, (SGU_CHUNK, SGU_CHUNK), 0)
    col = lax.broadcasted_iota(_I32, (SGU_CHUNK, SGU_CHUNK), 1)
    bst = bst_ref[...]
    cols = []
    for g in range(B_GROUPS):
        ws = jnp.where(row >= col, ws_ref[g], 0.0).astype(_BF16)
        bias = bst[:, g:g + 1]
        chunks = []
        for c in range(tm // SGU_CHUNK):
            blk = vn[c * SGU_CHUNK:(c + 1) * SGU_CHUNK, g * gdim:(g + 1) * gdim]
            chunks.append(jnp.dot(ws, blk, preferred_element_type=_F32) + bias)
        cols.append(jnp.concatenate(chunks, axis=0))
    mixed = jnp.concatenate(cols, axis=1)

    y = (mm(0) * mixed * _silu(mm(2 * width))).astype(_BF16)
    x2 = x + jnp.dot(y, wout_ref[...], preferred_element_type=_F32)
    o_ref[...] = _rms_normalize(x2, fg_ref[...])


def _sgu_layer(x2, g, w_in, ln_g, ln_b, w_s, b_s_t, w_out, final_g, tm):
    m, d = x2.shape
    width = w_out.shape[0]
    return pl.pallas_call(
        functools.partial(_sgu_kernel, tm=tm, width=width),
        grid=(m // tm,),
        in_specs=[
            pl.BlockSpec((tm, d), lambda i: (i, 0)),
            _const_spec((1, d)),
            _const_spec(w_in.shape),
            _const_spec((1, width)),
            _const_spec((1, width)),
            _const_spec(w_s.shape),
            _const_spec(b_s_t.shape),
            _const_spec(w_out.shape),
            _const_spec((1, d)),
        ],
        out_specs=pl.BlockSpec((tm, d), lambda i: (i, 0)),
        out_shape=jax.ShapeDtypeStruct((m, d), _F32),
        compiler_params=pltpu.CompilerParams(
            dimension_semantics=("arbitrary",), vmem_limit_bytes=VMEM_LIMIT),
        name="sgu_layer_final_norm",
    )(x2, g, w_in, ln_g, ln_b, w_s, b_s_t, w_out, final_g)


def _t5_bucket(rel):
    half = REL_BUCKETS // 2
    max_exact = half // 2
    ret = jnp.where(rel < 0, half, 0)
    n = jnp.abs(rel)
    nf = jnp.maximum(n, 1).astype(jnp.float32)
    large = max_exact + (jnp.log(nf / max_exact) / math.log(REL_MAX_DIST / max_exact)
                         * (half - max_exact)).astype(jnp.int32)
    large = jnp.minimum(large, half - 1)
    return ret + jnp.where(n < max_exact, n, large)


def _bias_tables(rel_bias):
    n = LANES
    far = rel_bias[_t5_bucket(jnp.int32(4 * REL_MAX_DIST))]

    def toeplitz(first_rel):
        rel = first_rel + jnp.arange(2 * n - 1, dtype=jnp.int32)
        vals = jnp.transpose((rel_bias[_t5_bucket(rel)] - far) * LOG2_E)
        hankel = jnp.tile(vals, (1, n + 1))[:, :n * 2 * n].reshape(-1, n, 2 * n)[:, :, :n]
        return hankel[:, ::-1, :]

    return toeplitz(-(n - 1)), toeplitz(1)


def kernel(x, norm_g, final_g, rel_bias, a_w_in, a_w_out, b_w_in, b_ln_g, b_ln_b, b_w_s,
           b_b_s, b_w_out):
    bsz, seq, d = x.shape
    assert norm_g.shape[0] == 2 and a_w_in.shape[0] == 1 and b_w_in.shape[0] == 1
    m = bsz * seq
    n_qi = IDX_HEADS * IDX_DIM
    topk = min(TOPK_MAX, seq // 4)

    cols = a_w_in.shape[2]
    w_pad = jnp.pad(a_w_in[0], ((0, 0), (0, 4 * A_WIDTH + n_qi + LANES - cols))).astype(_BF16)
    qk, q_t, v_t, z, qi_t, ki, wi_t = _proj_a(x, norm_g[0:1], w_pad, tm=512)
    words = _indexer(ki, qi_t, wi_t, tq=KEY_TILE, topk=topk)
    t0, t1 = _bias_tables(rel_bias)
    x1 = _attention(q_t, qk, v_t, words, t0, t1, z, x, a_w_out[0].astype(_BF16), tq=KEY_TILE)

    out = _sgu_layer(x1.reshape(m, d), norm_g[1:2], b_w_in[0].astype(_BF16),
                     b_ln_g[0:1], b_ln_b[0:1], b_w_s[0], jnp.transpose(b_b_s[0]),
                     b_w_out[0].astype(_BF16), final_g.reshape(1, d), tm=512)
    return out.reshape(bsz, seq, d)
```
